```python
import jax, jax.numpy as jnp
from jax import lax
import numpy as np

D_MODEL = 1024
BATCH = 8
SEQ = 4096
DEPTH = 2
DEC_BATCH = 32
DEC_SEQ = 4
PAST_LEN = 16384
PAGE_SIZE = 128

A_HEADS = 8
A_HEAD_DIM = 64
A_WIDTH = A_HEADS * A_HEAD_DIM
MOBA_BLOCK = 256
MOBA_TOPK = 3
MOBA_QBLOCK = 16
B_WIDTH = 512
B_BLOCKS = 8
B_BLOCK_DIM = B_WIDTH // B_BLOCKS
LRU_C = 8.0
CONV_WIDTH = 4
C_HEADS = 4
C_KEY_DIM = 128
C_VAL_DIM = 128
C_QK_WIDTH = C_HEADS * C_KEY_DIM
C_V_WIDTH = C_HEADS * C_VAL_DIM
C_CONV_DIM = 2 * C_QK_WIDTH + C_V_WIDTH
DELTA_CHUNK = 64
N_BRANCH = 3
IN_SIZES = (A_WIDTH, A_WIDTH, A_WIDTH, A_WIDTH, B_WIDTH, B_WIDTH, C_CONV_DIM, C_V_WIDTH, C_HEADS, C_HEADS, N_BRANCH * D_MODEL)
N_IN = 4 * A_WIDTH + 2 * B_WIDTH + C_CONV_DIM + C_V_WIDTH + 2 * C_HEADS + N_BRANCH * D_MODEL
EPS = 1e-6

kernel_name = "moba_rglru_gdn_gated_hybrid_step"

F32 = jnp.float32


def rms_norm(x, w):
    xf = x.astype(F32)
    y = xf * lax.rsqrt(jnp.mean(xf * xf, axis=-1, keepdims=True) + EPS) * w.astype(F32)
    return y.astype(x.dtype)


def l2_normalize(x):
    return x * lax.rsqrt(jnp.sum(x * x, axis=-1, keepdims=True) + EPS)


def split_cols(u):
    outs = []
    off = 0
    for s in IN_SIZES:
        outs.append(u[..., off:off + s])
        off += s
    return outs


def causal_conv(x, buf, w):
    L = x.shape[1]
    xp = jnp.concatenate([buf, x], axis=1)
    y = xp[:, 0:L] * w[0]
    for j in range(1, CONV_WIDTH):
        y = y + xp[:, j:j + L] * w[j]
    return y, xp[:, L:]


def alibi_slopes(n_heads):
    return jnp.exp2(-8.0 * jnp.arange(1, n_heads + 1, dtype=F32) / n_heads)


def moba_group_attend(q, t_pos, kb, vb, means, slopes):
    Bn, Q, H, hd = q.shape
    nbg = means.shape[2]
    b_t = t_pos // MOBA_BLOCK
    gate = jnp.einsum('bqhd,bhnd->bhqn', q, means)
    past = jnp.arange(nbg)[None, :] < b_t[:, None]
    gate = jnp.where(past[None, None], gate, -jnp.inf)
    _, sel = lax.top_k(gate, MOBA_TOPK)
    own = jnp.broadcast_to(b_t[None, None, :, None], (Bn, H, Q, 1))
    blk = jnp.concatenate([own, sel.astype(own.dtype)], axis=-1)
    slot_ok = jnp.concatenate([jnp.ones((Q, 1), bool), jnp.arange(MOBA_TOPK)[None, :] < b_t[:, None]], axis=-1)
    bi = jnp.arange(Bn)[:, None, None, None]
    hi = jnp.arange(H)[None, :, None, None]
    kg = kb[bi, hi, blk]
    vg = vb[bi, hi, blk]
    s = jnp.einsum('bqhd,bhqjkd->bhqjk', q, kg)
    kpos = blk[..., None] * MOBA_BLOCK + jnp.arange(MOBA_BLOCK)
    dist = t_pos[None, None, :, None, None] - kpos
    valid = (dist >= 0) & slot_ok[None, None, :, :, None]
    s = jnp.where(valid, s - slopes[None, :, None, None, None] * dist.astype(F32), -jnp.inf)
    p = jax.nn.softmax(s.reshape(Bn, H, Q, -1), axis=-1).reshape(s.shape)
    return jnp.einsum('bhqjk,bhqjkd->bqhd', p, vg)


def moba_attention(q, k, v, k_past, v_past):
    Bn, L, H, hd = q.shape
    P = k_past.shape[1]
    T = P + L
    k_all = jnp.concatenate([k_past.astype(F32), k.astype(F32)], axis=1)
    v_all = jnp.concatenate([v_past.astype(F32), v.astype(F32)], axis=1)
    nbg = max(-(-T // MOBA_BLOCK), MOBA_TOPK)

    def blockify(t):
        t = jnp.pad(t, ((0, 0), (0, nbg * MOBA_BLOCK - T), (0, 0), (0, 0)))
        return t.reshape(Bn, nbg, MOBA_BLOCK, H, hd).transpose(0, 3, 1, 2, 4)

    kb = blockify(k_all)
    vb = blockify(v_all)
    means = jnp.mean(kb, axis=3)
    qs = q.astype(F32) * (hd ** -0.5)
    pos = P + jnp.arange(L, dtype=jnp.int32)
    slopes = alibi_slopes(H)

    def attend(qp):
        return moba_group_attend(qp[0], qp[1], kb, vb, means, slopes)

    if L > MOBA_QBLOCK and L % MOBA_QBLOCK == 0:
        nq = L // MOBA_QBLOCK
        qch = qs.reshape(Bn, nq, MOBA_QBLOCK, H, hd).transpose(1, 0, 2, 3, 4)
        pch = pos.reshape(nq, MOBA_QBLOCK)
        o = lax.map(attend, (qch, pch))
        o = o.transpose(1, 0, 2, 3, 4).reshape(Bn, L, H, hd)
    else:
        o = attend((qs, pos))
    return o


def linear_scan(a, b, h0):
    def comb(l, r):
        return (l[0] * r[0], r[0] * l[1] + r[1])
    A, Bc = lax.associative_scan(comb, (a, b), axis=1)
    h = A * h0[:, None, :] + Bc
    return h, h[:, -1]


def rglru_branch(xb, zb, conv0, h0, conv_w, conv_b, wa, ba, wx, bx, lam):
    Bn, L, _ = xb.shape
    xc, conv1 = causal_conv(xb.astype(F32), conv0.astype(F32), conv_w.astype(F32))
    xc = xc + conv_b.astype(F32)
    xblk = xc.reshape(Bn, L, B_BLOCKS, B_BLOCK_DIM)
    r = jax.nn.sigmoid(jnp.einsum('blgi,gij->blgj', xblk, wa.astype(F32)).reshape(Bn, L, B_WIDTH) + ba.astype(F32))
    i = jax.nn.sigmoid(jnp.einsum('blgi,gij->blgj', xblk, wx.astype(F32)).reshape(Bn, L, B_WIDTH) + bx.astype(F32))
    log_a = -LRU_C * r * jax.nn.softplus(-lam.astype(F32))
    a = jnp.exp(log_a)
    bin_ = jnp.sqrt(-jnp.expm1(2.0 * log_a)) * (i * xc)
    hs, h_last = linear_scan(a, bin_, h0.astype(F32))
    y = jax.nn.silu(zb.astype(F32)) * hs
    return y, conv1.astype(conv0.dtype), h_last.astype(h0.dtype)


def chunked_gated_delta(q, k, v, beta, g, s0):
    Bn, L, H, dk = q.shape
    C = min(DELTA_CHUNK, L)
    n = -(-L // C)
    pad = n * C - L

    def prep(t):
        t = jnp.pad(t, [(0, 0), (0, pad)] + [(0, 0)] * (t.ndim - 2))
        t = t.reshape((Bn, n, C) + t.shape[2:])
        t = jnp.moveaxis(t, 3, 2)
        return jnp.moveaxis(t, 1, 0)

    qc, kc, vc, bc, gc = prep(q), prep(k), prep(v), prep(beta), prep(g)
    G = jnp.cumsum(gc, axis=-1)
    idx = jnp.arange(C)
    causal = idx[:, None] >= idx[None, :]
    strict = idx[:, None] > idx[None, :]
    decay = jnp.exp(jnp.where(causal, G[..., :, None] - G[..., None, :], -jnp.inf))
    kb = kc * bc[..., None]
    A = jnp.where(strict, jnp.einsum('nbhid,nbhjd->nbhij', kb, kc) * decay, 0.0)
    eye = jnp.eye(C, dtype=F32)
    Tm = lax.linalg.triangular_solve(eye + A, jnp.broadcast_to(eye, A.shape), left_side=True, lower=True)
    U = Tm @ (vc * bc[..., None])
    W = Tm @ (kb * jnp.exp(G)[..., None])
    QK = jnp.einsum('nbhid,nbhjd->nbhij', qc, kc) * decay
    qg = qc * jnp.exp(G)[..., None]
    kg = kc * jnp.exp(G[..., -1:] - G)[..., None]
    gl = jnp.exp(G[..., -1])

    def step(S, xs):
        qg_i, kg_i, U_i, W_i, QK_i, gl_i = xs
        vn = U_i - W_i @ S
        o = qg_i @ S + QK_i @ vn
        S = S * gl_i[..., None, None] + jnp.einsum('bhck,bhcv->bhkv', kg_i, vn)
        return S, o

    S, o = lax.scan(step, s0, (qg, kg, U, W, QK, gl))
    o = o.transpose(1, 0, 3, 2, 4).reshape(Bn, n * C, H, -1)[:, :L]
    return o, S


def gated_delta_branch(qkv, b_raw, a_raw, z, conv0, s0, conv_w, a_log, dt_bias, norm_w):
    Bn, L, _ = qkv.shape
    y, conv1 = causal_conv(qkv.astype(F32), conv0.astype(F32), conv_w.astype(F32))
    y = jax.nn.silu(y)
    q = l2_normalize(y[..., :C_QK_WIDTH].reshape(Bn, L, C_HEADS, C_KEY_DIM)) * (C_KEY_DIM ** -0.5)
    k = l2_normalize(y[..., C_QK_WIDTH:2 * C_QK_WIDTH].reshape(Bn, L, C_HEADS, C_KEY_DIM))
    v = y[..., 2 * C_QK_WIDTH:].reshape(Bn, L, C_HEADS, C_VAL_DIM)
    beta = jax.nn.sigmoid(b_raw.astype(F32))
    g = -jnp.exp(a_log.astype(F32)) * jax.nn.softplus(a_raw.astype(F32) + dt_bias.astype(F32))
    o, s1 = chunked_gated_delta(q, k, v, beta, g, s0.astype(F32))
    o = o * lax.rsqrt(jnp.mean(o * o, axis=-1, keepdims=True) + EPS) * norm_w.astype(F32)
    o = o * jax.nn.silu(z.astype(F32).reshape(Bn, L, C_HEADS, C_VAL_DIM))
    return o.reshape(Bn, L, C_V_WIDTH), conv1.astype(conv0.dtype), s1.astype(s0.dtype)


def trunk_layer(h, k_past, v_past, conv_b0, h_b0, conv_c0, s_c0,
                norm_w, w_in, conv_b_w, conv_b_b, lru_wa, lru_ba, lru_wx, lru_bx, lru_lambda,
                conv_c_w, dn_a_log, dn_dt_bias, dn_norm_w, w_pa, w_pb, w_pc, w_out):
    dt = h.dtype
    Bn, L, _ = h.shape
    xn = rms_norm(h, norm_w)
    u = xn @ w_in
    qa, ka, va, za, xb, zb, qkvc, zc, bc, ac, gates = split_cols(u)
    qa = qa.reshape(Bn, L, A_HEADS, A_HEAD_DIM)
    ka = ka.reshape(Bn, L, A_HEADS, A_HEAD_DIM)
    va = va.reshape(Bn, L, A_HEADS, A_HEAD_DIM)
    oa = moba_attention(qa, ka, va, k_past, v_past).reshape(Bn, L, A_WIDTH)
    ya = (jax.nn.silu(za.astype(F32)) * oa).astype(dt) @ w_pa
    ob, conv_b1, h_b1 = rglru_branch(xb, zb, conv_b0, h_b0, conv_b_w, conv_b_b, lru_wa, lru_ba, lru_wx, lru_bx, lru_lambda)
    yb = ob.astype(dt) @ w_pb
    oc, conv_c1, s_c1 = gated_delta_branch(qkvc, bc, ac, zc, conv_c0, s_c0, conv_c_w, dn_a_log, dn_dt_bias, dn_norm_w)
    yc = oc.astype(dt) @ w_pc
    gs = jax.nn.sigmoid(gates.astype(F32)).reshape(Bn, L, N_BRANCH, D_MODEL)
    merged = gs[:, :, 0] * ya.astype(F32) + gs[:, :, 1] * yb.astype(F32) + gs[:, :, 2] * yc.astype(F32)
    h = h + merged.astype(dt) @ w_out
    return h, ka, va, conv_b1, h_b1, conv_c1, s_c1


def setup_inputs(seed: int = 0) -> dict:
    key = jax.random.key(seed)
    ks = jax.random.split(key, 32)
    n_pages = PAST_LEN // PAGE_SIZE
    n_used = DEC_BATCH * n_pages
    n_pool = n_used + (n_used + 3) // 4

    def nrm(k, shape, scale):
        return jax.random.normal(k, shape, F32) * scale

    x_prompt = nrm(ks[0], (BATCH, SEQ, D_MODEL), 1.0)
    x_sample = nrm(ks[1], (DEC_BATCH, DEC_SEQ, D_MODEL), 1.0)
    cache_k = nrm(ks[2], (DEPTH, n_pool, PAGE_SIZE, A_HEADS, A_HEAD_DIM), 1.0)
    cache_v = nrm(ks[3], (DEPTH, n_pool, PAGE_SIZE, A_HEADS, A_HEAD_DIM), 1.0)
    page_table = jax.random.permutation(ks[4], n_pool)[:n_used].reshape(DEC_BATCH, n_pages).astype(jnp.int32)
    state_conv_b = nrm(ks[5], (DEPTH, DEC_BATCH, CONV_WIDTH - 1, B_WIDTH), 1.0)
    state_h_b = nrm(ks[6], (DEPTH, DEC_BATCH, B_WIDTH), 0.5)
    state_conv_c = nrm(ks[7], (DEPTH, DEC_BATCH, CONV_WIDTH - 1, C_CONV_DIM), 1.0)
    state_s_c = nrm(ks[8], (DEPTH, DEC_BATCH, C_HEADS, C_KEY_DIM, C_VAL_DIM), 0.1)

    norm_w = 1.0 + nrm(ks[9], (DEPTH, D_MODEL), 0.02)
    w_in = nrm(ks[10], (DEPTH, D_MODEL, N_IN), D_MODEL ** -0.5)
    conv_b_w = nrm(ks[11], (DEPTH, CONV_WIDTH, B_WIDTH), CONV_WIDTH ** -0.5)
    conv_b_b = nrm(ks[12], (DEPTH, B_WIDTH), 0.02)
    lru_wa = nrm(ks[13], (DEPTH, B_BLOCKS, B_BLOCK_DIM, B_BLOCK_DIM), B_BLOCK_DIM ** -0.5)
    lru_ba = nrm(ks[14], (DEPTH, B_WIDTH), 0.02)
    lru_wx = nrm(ks[15], (DEPTH, B_BLOCKS, B_BLOCK_DIM, B_BLOCK_DIM), B_BLOCK_DIM ** -0.5)
    lru_bx = nrm(ks[16], (DEPTH, B_WIDTH), 0.02)
    a0 = jax.random.uniform(ks[17], (DEPTH, B_WIDTH), F32, 0.9, 0.999)
    lru_lambda = jnp.log(a0) - jnp.log1p(-a0)
    conv_c_w = nrm(ks[18], (DEPTH, CONV_WIDTH, C_CONV_DIM), CONV_WIDTH ** -0.5)
    dn_a_log = jnp.log(jax.random.uniform(ks[19], (DEPTH, C_HEADS), F32, 1.0, 16.0))
    dt0 = jnp.exp(jax.random.uniform(ks[20], (DEPTH, C_HEADS), F32, np.log(1e-3), np.log(1e-1)))
    dn_dt_bias = dt0 + jnp.log(-jnp.expm1(-dt0))
    dn_norm_w = 1.0 + nrm(ks[21], (DEPTH, C_VAL_DIM), 0.02)
    w_pa = nrm(ks[22], (DEPTH, A_WIDTH, D_MODEL), A_WIDTH ** -0.5)
    w_pb = nrm(ks[23], (DEPTH, B_WIDTH, D_MODEL), B_WIDTH ** -0.5)
    w_pc = nrm(ks[24], (DEPTH, C_V_WIDTH, D_MODEL), C_V_WIDTH ** -0.5)
    w_out = nrm(ks[25], (DEPTH, D_MODEL, D_MODEL), D_MODEL ** -0.5)
    final_norm_w = 1.0 + nrm(ks[26], (D_MODEL,), 0.02)
    return {"x_prompt": x_prompt, "x_sample": x_sample, "cache_k": cache_k, "cache_v": cache_v,
            "page_table": page_table, "state_conv_b": state_conv_b, "state_h_b": state_h_b,
            "state_conv_c": state_conv_c, "state_s_c": state_s_c,
            "norm_w": norm_w, "w_in": w_in, "conv_b_w": conv_b_w, "conv_b_b": conv_b_b,
            "lru_wa": lru_wa, "lru_ba": lru_ba, "lru_wx": lru_wx, "lru_bx": lru_bx, "lru_lambda": lru_lambda,
            "conv_c_w": conv_c_w, "dn_a_log": dn_a_log, "dn_dt_bias": dn_dt_bias, "dn_norm_w": dn_norm_w,
            "w_pa": w_pa, "w_pb": w_pb, "w_pc": w_pc, "w_out": w_out, "final_norm_w": final_norm_w}


def reference(x_prompt, x_sample, cache_k, cache_v, page_table, state_conv_b, state_h_b, state_conv_c, state_s_c,
              norm_w, w_in, conv_b_w, conv_b_b, lru_wa, lru_ba, lru_wx, lru_bx, lru_lambda,
              conv_c_w, dn_a_log, dn_dt_bias, dn_norm_w, w_pa, w_pb, w_pc, w_out, final_norm_w):
    dt = x_prompt.dtype
    Bp = x_prompt.shape[0]
    Bs = x_sample.shape[0]
    n_pages = page_table.shape[1]
    empty_kv = jnp.zeros((Bp, 0, A_HEADS, A_HEAD_DIM), dt)
    zero_conv_b = jnp.zeros((Bp, CONV_WIDTH - 1, B_WIDTH), dt)
    zero_h_b = jnp.zeros((Bp, B_WIDTH), dt)
    zero_conv_c = jnp.zeros((Bp, CONV_WIDTH - 1, C_CONV_DIM), dt)
    zero_s_c = jnp.zeros((Bp, C_HEADS, C_KEY_DIM, C_VAL_DIM), dt)
    hp = x_prompt
    hs = x_sample
    kp_l, vp_l, ks_l, vs_l = [], [], [], []
    cbp_l, hbp_l, cbs_l, hbs_l = [], [], [], []
    ccp_l, scp_l, ccs_l, scs_l = [], [], [], []
    for l in range(DEPTH):
        lw = (norm_w[l], w_in[l], conv_b_w[l], conv_b_b[l], lru_wa[l], lru_ba[l], lru_wx[l], lru_bx[l],
              lru_lambda[l], conv_c_w[l], dn_a_log[l], dn_dt_bias[l], dn_norm_w[l],
              w_pa[l], w_pb[l], w_pc[l], w_out[l])
        hp, kp, vp, cbp, hbp, ccp, scp = trunk_layer(hp, empty_kv, empty_kv, zero_conv_b, zero_h_b,
                                                     zero_conv_c, zero_s_c, *lw)
        k_past = cache_k[l, page_table].reshape(Bs, n_pages * PAGE_SIZE, A_HEADS, A_HEAD_DIM)
        v_past = cache_v[l, page_table].reshape(Bs, n_pages * PAGE_SIZE, A_HEADS, A_HEAD_DIM)
        hs, ks_, vs_, cbs, hbs, ccs, scs = trunk_layer(hs, k_past, v_past, state_conv_b[l], state_h_b[l],
                                                       state_conv_c[l], state_s_c[l], *lw)
        kp_l.append(kp); vp_l.append(vp); ks_l.append(ks_); vs_l.append(vs_)
        cbp_l.append(cbp); hbp_l.append(hbp); cbs_l.append(cbs); hbs_l.append(hbs)
        ccp_l.append(ccp); scp_l.append(scp); ccs_l.append(ccs); scs_l.append(scs)
    y_prompt = rms_norm(hp, final_norm_w)
    y_sample = rms_norm(hs, final_norm_w)
    return (y_prompt, y_sample,
            jnp.stack(kp_l), jnp.stack(vp_l), jnp.stack(ks_l), jnp.stack(vs_l),
            jnp.stack(cbp_l), jnp.stack(hbp_l), jnp.stack(cbs_l), jnp.stack(hbs_l),
            jnp.stack(ccp_l), jnp.stack(scp_l), jnp.stack(ccs_l), jnp.stack(scs_l))
```

```python
import functools

import jax
import jax.numpy as jnp
from jax import lax
from jax.experimental import pallas as pl
from jax.experimental.pallas import tpu as pltpu

F32 = jnp.float32
BF16 = jnp.bfloat16
HIGHEST = lax.Precision.HIGHEST

D_MODEL = 1024
A_HEADS = 8
A_HEAD_DIM = 64
A_WIDTH = A_HEADS * A_HEAD_DIM
MOBA_BLOCK = 256
MOBA_TOPK = 3
PAGE_SIZE = 128
B_WIDTH = 512
B_BLOCKS = 8
LRU_C = 8.0
CONV_WIDTH = 4
C_HEADS = 4
C_KEY_DIM = 128
C_VAL_DIM = 128
C_QK_WIDTH = C_HEADS * C_KEY_DIM
C_V_WIDTH = C_HEADS * C_VAL_DIM
C_CONV_DIM = 2 * C_QK_WIDTH + C_V_WIDTH
DELTA_CHUNK = 64
N_BRANCH = 3
EPS = 1e-6

COL = 512
CB_QA, CB_KA, CB_VA, CB_ZA, CB_XB, CB_ZB, CB_QKVC, CB_ZC, CB_GATES, CB_BETA, CB_ALPHA = 0, 1, 2, 3, 4, 5, 6, 9, 10, 16, 17
N_COLBLK = 18
N_U = N_COLBLK * COL

SUBLANES = 8
TAIL0 = SUBLANES - (CONV_WIDTH - 1)
VMEM_LIMIT = 56 * 1024 * 1024

NT_DIMS = (((1,), (1,)), ((), ()))
TN_DIMS = (((0,), (0,)), ((), ()))


def _cparams(*sem):
    return pltpu.CompilerParams(dimension_semantics=sem, vmem_limit_bytes=VMEM_LIMIT)


def _sigmoid(x):
    return 1.0 / (1.0 + jnp.exp(-x))


def _silu(x):
    return x * _sigmoid(x)


def _softplus(x):
    return jnp.maximum(x, 0.0) + jnp.log1p(jnp.exp(-jnp.abs(x)))


def _inproj_kernel(h_ref, nw_ref, w_ref, u_ref, xn_sc):
    @pl.when(pl.program_id(1) == 0)
    def _():
        x = h_ref[...]
        y = x * lax.rsqrt(jnp.mean(x * x, axis=-1, keepdims=True) + EPS) * nw_ref[...]
        xn_sc[...] = y.astype(BF16)

    u_ref[...] = jnp.dot(xn_sc[...], w_ref[...], preferred_element_type=F32)


def _inproj(h2d, norm_w, w_re, tm):
    T = h2d.shape[0]
    return pl.pallas_call(
        _inproj_kernel,
        grid=(T // tm, N_COLBLK),
        in_specs=[pl.BlockSpec((tm, D_MODEL), lambda i, j: (i, 0)),
                  pl.BlockSpec((1, D_MODEL), lambda i, j: (0, 0)),
                  pl.BlockSpec((D_MODEL, COL), lambda i, j: (0, j))],
        out_specs=pl.BlockSpec((tm, COL), lambda i, j: (i, j)),
        out_shape=jax.ShapeDtypeStruct((T, N_U), F32),
        scratch_shapes=[pltpu.VMEM((tm, D_MODEL), BF16)],
        compiler_params=_cparams("parallel", "arbitrary"),
        name="inproj",
    )(h2d, norm_w.reshape(1, D_MODEL), w_re)


def _topk_select(gate, n_past, n_iota):
    nb = gate.shape[0]
    rank = jnp.zeros_like(gate)
    for m in range(nb):
        gm = gate[m:m + 1, :]
        beats = jnp.where(gm > gate, 1.0, 0.0) + jnp.where(gm == gate, jnp.where(n_iota > m, 1.0, 0.0), 0.0)
        rank = rank + beats * jnp.where(m < n_past, 1.0, 0.0)
    return jnp.where(n_iota < n_past, jnp.where(rank < MOBA_TOPK, 1.0, 0.0), 0.0)


def _moba_prompt_kernel(slopes_ref, q_ref, k_ref, v_ref, o_ref, kb_sc, vt_sc, means_sc, cr_sc, sel_sc, *, nblk):
    BLK = MOBA_BLOCK
    pair = pl.program_id(1)
    i = pl.program_id(2)

    @pl.when(i == 0)
    def _init():
        for n in range(nblk):
            kblk = k_ref[n * BLK:(n + 1) * BLK, :]
            means_sc[n:n + 1, :] = jnp.sum(kblk, axis=0, keepdims=True) * (1.0 / BLK)
            kb_sc[n] = kblk.astype(BF16)
            vt_sc[n] = v_ref[n * BLK:(n + 1) * BLK, :].T.astype(BF16)
        c_io = lax.broadcasted_iota(jnp.int32, (BLK, BLK), 0)
        r_io = lax.broadcasted_iota(jnp.int32, (BLK, BLK), 1)
        cr_sc[...] = (c_io - r_io).astype(F32)

    qf = q_ref[...] * (A_HEAD_DIM ** -0.5)
    lane = lax.broadcasted_iota(jnp.int32, (1, 2 * A_HEAD_DIM), 1)
    n_iota = lax.broadcasted_iota(jnp.int32, (nblk, BLK), 0)
    cr = cr_sc[...]
    outs = []
    for hh in range(2):
        slope = slopes_ref[pair * 2 + hh]
        qh = jnp.where((lane // A_HEAD_DIM) == hh, qf, 0.0)
        gate = lax.dot_general(means_sc[...], qh, NT_DIMS, precision=HIGHEST, preferred_element_type=F32)
        sel = _topk_select(gate, i, n_iota)
        for n in range(nblk):
            sel_sc[n] = sel[n:n + 1, :]
        qh16 = qh.astype(BF16)
        vrows = slice(hh * A_HEAD_DIM, (hh + 1) * A_HEAD_DIM)

        s = lax.dot_general(kb_sc[i], qh16, NT_DIMS, preferred_element_type=F32)
        s = jnp.where(cr <= 0.0, s + slope * cr, -jnp.inf)
        m0 = jnp.max(s, axis=0, keepdims=True)
        p = jnp.exp(s - m0)
        l0 = jnp.sum(p, axis=0, keepdims=True)
        acc0 = jnp.dot(vt_sc[i, vrows, :], p.astype(BF16), preferred_element_type=F32)

        def body(j, carry):
            m, l, acc = carry
            s = lax.dot_general(kb_sc[j], qh16, NT_DIMS, preferred_element_type=F32)
            negdist = cr - ((i - j) * BLK).astype(F32)
            s = jnp.where(sel_sc[j] > 0.0, s + slope * negdist, -jnp.inf)
            m_new = jnp.maximum(m, jnp.max(s, axis=0, keepdims=True))
            alpha = jnp.exp(m - m_new)
            p = jnp.exp(s - m_new)
            l = alpha * l + jnp.sum(p, axis=0, keepdims=True)
            acc = acc * alpha + jnp.dot(vt_sc[j, vrows, :], p.astype(BF16), preferred_element_type=F32)
            return m_new, l, acc

        _, l, acc = lax.fori_loop(0, i, body, (m0, l0, acc0))
        outs.append(acc / l)
    o_ref[...] = jnp.concatenate(outs, axis=0).T


def _moba_prompt(u, slopes, Bn, L):
    nblk = L // MOBA_BLOCK
    PW = 2 * A_HEAD_DIM
    cpb = COL // PW
    return pl.pallas_call(
        functools.partial(_moba_prompt_kernel, nblk=nblk),
        grid=(Bn, A_HEADS // 2, nblk),
        in_specs=[pl.BlockSpec(memory_space=pltpu.SMEM),
                  pl.BlockSpec((MOBA_BLOCK, PW), lambda b, p, i: (b * nblk + i, CB_QA * cpb + p)),
                  pl.BlockSpec((L, PW), lambda b, p, i: (b, CB_KA * cpb + p)),
                  pl.BlockSpec((L, PW), lambda b, p, i: (b, CB_VA * cpb + p))],
        out_specs=pl.BlockSpec((MOBA_BLOCK, PW), lambda b, p, i: (b * nblk + i, p)),
        out_shape=jax.ShapeDtypeStruct((Bn * L, A_WIDTH), F32),
        scratch_shapes=[pltpu.VMEM((nblk, MOBA_BLOCK, PW), BF16),
                        pltpu.VMEM((nblk, PW, MOBA_BLOCK), BF16),
                        pltpu.VMEM((nblk, PW), F32),
                        pltpu.VMEM((MOBA_BLOCK, MOBA_BLOCK), F32),
                        pltpu.VMEM((nblk, 1, MOBA_BLOCK), F32)],
        compiler_params=_cparams("parallel", "parallel", "arbitrary"),
        name="moba_prompt",
    )(slopes, u, u, u)


PAGES_PER_STEP = 16
HQ = 128


def _moba_sample_kernel(pt_ref, slope_ref, tq_ref, q_ref, kn_ref, vn_ref, *refs, n_pages, dec_seq, layer):
    del pt_ref, layer
    PPS = PAGES_PER_STEP
    k_refs = refs[:PPS]
    v_refs = refs[PPS:2 * PPS]
    o_ref = refs[2 * PPS]
    qbd_sc, s_sc, p_sc, acc_sc, pown_sc, vown_sc, gate_sc = refs[2 * PPS + 1:]
    ph = pl.program_id(1)
    st = pl.program_id(2)
    nsteps = n_pages // PPS
    P = n_pages * PAGE_SIZE
    nb = P // MOBA_BLOCK
    head_of_row = lax.broadcasted_iota(jnp.int32, (A_HEADS, A_WIDTH), 0)
    head_of_lane = lax.broadcasted_iota(jnp.int32, (A_HEADS, A_WIDTH), 1) // A_HEAD_DIM
    hmask = jnp.where(head_of_row == head_of_lane, 1.0, 0.0)

    @pl.when((ph == 0) & (st == 0))
    def _init():
        qbd_sc[...] = jnp.zeros_like(qbd_sc)
        for t in range(dec_seq):
            qrow = q_ref[t:t + 1, :] * (A_HEAD_DIM ** -0.5)
            qbd_sc[t * A_HEADS:(t + 1) * A_HEADS, :] = (qrow * hmask).astype(BF16)

    @pl.when(ph == 0)
    def _scores():
        for r in range(PPS):
            kpage = k_refs[r][...].astype(BF16)
            row0 = pl.multiple_of((st * PPS + r) * PAGE_SIZE, PAGE_SIZE)
            s_sc[pl.ds(row0, PAGE_SIZE), :] = lax.dot_general(kpage, qbd_sc[...], NT_DIMS,
                                                              preferred_element_type=F32)

    @pl.when((ph == 0) & (st == nsteps - 1))
    def _softmax():
        slope = slope_ref[...]
        tq = tq_ref[...]
        lane = lax.broadcasted_iota(jnp.int32, (1, HQ), 1)
        for n in range(nb):
            gate_sc[n:n + 1, :] = jnp.sum(s_sc[n * MOBA_BLOCK:(n + 1) * MOBA_BLOCK, :], axis=0, keepdims=True)
        gate = gate_sc[...] * (1.0 / MOBA_BLOCK)
        n_iota = lax.broadcasted_iota(jnp.int32, (nb, HQ), 0)
        sel = _topk_select(gate, nb, n_iota)
        t_io = lax.broadcasted_iota(jnp.int32, (SUBLANES, HQ), 0)
        q_of_lane = lane // A_HEADS
        s_own = lax.dot_general(kn_ref[...].astype(BF16), qbd_sc[...], NT_DIMS, preferred_element_type=F32)
        own_ok = (t_io <= q_of_lane) & (t_io < dec_seq)
        s_own = jnp.where(own_ok, s_own - slope * (q_of_lane - t_io).astype(F32), -jnp.inf)
        m = jnp.max(s_own, axis=0, keepdims=True)
        kpos = lax.broadcasted_iota(jnp.int32, (MOBA_BLOCK, HQ), 0).astype(F32)
        for n in range(nb):
            sb = s_sc[n * MOBA_BLOCK:(n + 1) * MOBA_BLOCK, :]
            sb = sb - slope * (tq - (kpos + float(n * MOBA_BLOCK)))
            sb = jnp.where(sel[n:n + 1, :] > 0.0, sb, -jnp.inf)
            s_sc[n * MOBA_BLOCK:(n + 1) * MOBA_BLOCK, :] = sb
            m = jnp.maximum(m, jnp.max(sb, axis=0, keepdims=True))
        p_own = jnp.exp(s_own - m)
        l = jnp.sum(p_own, axis=0, keepdims=True)
        for n in range(nb):
            pb = jnp.exp(s_sc[n * MOBA_BLOCK:(n + 1) * MOBA_BLOCK, :] - m)
            s_sc[n * MOBA_BLOCK:(n + 1) * MOBA_BLOCK, :] = pb
            l = l + jnp.sum(pb, axis=0, keepdims=True)
        inv_l = 1.0 / l
        for n in range(nb):
            p_sc[n * MOBA_BLOCK:(n + 1) * MOBA_BLOCK, :] = (
                s_sc[n * MOBA_BLOCK:(n + 1) * MOBA_BLOCK, :] * inv_l).astype(BF16)
        pown_sc[...] = jnp.zeros_like(pown_sc)
        pown_sc[0:SUBLANES, :] = p_own * inv_l
        vown_sc[...] = jnp.zeros_like(vown_sc)
        vown_sc[0:SUBLANES, :] = vn_ref[...]
        acc_sc[...] = lax.dot_general(pown_sc[...].astype(BF16), vown_sc[...].astype(BF16), TN_DIMS,
                                      preferred_element_type=F32)

    @pl.when(ph == 1)
    def _values():
        acc = acc_sc[...]
        for r in range(PPS):
            row0 = pl.multiple_of((st * PPS + r) * PAGE_SIZE, PAGE_SIZE)
            acc = acc + lax.dot_general(p_sc[pl.ds(row0, PAGE_SIZE), :], v_refs[r][...].astype(BF16), TN_DIMS,
                                        preferred_element_type=F32)
        acc_sc[...] = acc

    @pl.when((ph == 1) & (st == nsteps - 1))
    def _finish():
        o_ref[...] = jnp.zeros_like(o_ref)
        for t in range(dec_seq):
            blk = acc_sc[t * A_HEADS:(t + 1) * A_HEADS, :] * hmask
            o_ref[t:t + 1, :] = jnp.sum(blk, axis=0, keepdims=True)


def _moba_sample(u_s, cache_k, cache_v, page_table, slope_lane, tq_lane, layer, Bs, tpad, dec_seq):
    n_pages = page_table.shape[1]
    PPS = PAGES_PER_STEP
    nsteps = n_pages // PPS
    P = n_pages * PAGE_SIZE
    u3 = u_s.reshape(Bs, tpad, N_U)

    def kmap(r):
        return lambda b, ph, st, pt: (layer, pt[b, (st * (1 - ph) + (nsteps - 1) * ph) * PPS + r], 0, 0)

    def vmap_(r):
        return lambda b, ph, st, pt: (layer, pt[b, (st * ph) * PPS + r], 0, 0)

    page_blk = (None, None, PAGE_SIZE, A_WIDTH)
    in_specs = [pl.BlockSpec((1, HQ), lambda b, ph, st, pt: (0, 0)),
                pl.BlockSpec((1, HQ), lambda b, ph, st, pt: (0, 0)),
                pl.BlockSpec((None, tpad, COL), lambda b, ph, st, pt: (b, 0, CB_QA)),
                pl.BlockSpec((None, tpad, COL), lambda b, ph, st, pt: (b, 0, CB_KA)),
                pl.BlockSpec((None, tpad, COL), lambda b, ph, st, pt: (b, 0, CB_VA))]
    in_specs += [pl.BlockSpec(page_blk, kmap(r)) for r in range(PPS)]
    in_specs += [pl.BlockSpec(page_blk, vmap_(r)) for r in range(PPS)]
    grid_spec = pltpu.PrefetchScalarGridSpec(
        num_scalar_prefetch=1,
        grid=(Bs, 2, nsteps),
        in_specs=in_specs,
        out_specs=pl.BlockSpec((None, tpad, A_WIDTH), lambda b, ph, st, pt: (b, 0, 0)),
        scratch_shapes=[pltpu.VMEM((HQ, A_WIDTH), BF16),
                        pltpu.VMEM((P, HQ), F32),
                        pltpu.VMEM((P, HQ), BF16),
                        pltpu.VMEM((HQ, A_WIDTH), F32),
                        pltpu.VMEM((PAGE_SIZE, HQ), F32),
                        pltpu.VMEM((PAGE_SIZE, A_WIDTH), F32),
                        pltpu.VMEM((P // MOBA_BLOCK, HQ), F32)])
    out = pl.pallas_call(
        functools.partial(_moba_sample_kernel, n_pages=n_pages, dec_seq=dec_seq, layer=layer),
        grid_spec=grid_spec,
        out_shape=jax.ShapeDtypeStruct((Bs, tpad, A_WIDTH), F32),
        compiler_params=_cparams("parallel", "arbitrary", "arbitrary"),
        name="moba_sample",
    )(page_table, slope_lane, tq_lane, u3, u3, u3, *([cache_k] * PPS), *([cache_v] * PPS))
    return out.reshape(Bs * tpad, A_WIDTH)


def _lru_kernel(xb_ref, zb_ref, conv0_ref, h0_ref, cw_ref, cb_ref, wa_ref, ba_ref, wx_ref, bx_ref, lam_ref,
                y_ref, conv1_ref, h1_ref, xp_sc, h_sc, a_sc, b_sc, *, Bn, tc, l_valid):
    c = pl.program_id(0)

    @pl.when(c == 0)
    def _():
        xp_sc[:, TAIL0:SUBLANES, :] = conv0_ref[...]
        h_sc[...] = h0_ref[...]

    xp_sc[:, SUBLANES:SUBLANES + tc, :] = xb_ref[...]
    xc = xp_sc[:, TAIL0:TAIL0 + tc, :] * cw_ref[0:1, :]
    for j in range(1, CONV_WIDTH):
        xc = xc + xp_sc[:, TAIL0 + j:TAIL0 + j + tc, :] * cw_ref[j:j + 1, :]
    xc = xc + cb_ref[...]
    nv = min(tc, l_valid)
    tail = xp_sc[:, TAIL0 + nv:SUBLANES + nv, :]
    conv1_ref[...] = tail
    xp_sc[:, TAIL0:SUBLANES, :] = tail

    x2 = xc.reshape(Bn * tc, B_WIDTH)
    x16 = x2.astype(BF16)
    r = _sigmoid(jnp.dot(x16, wa_ref[...], preferred_element_type=F32) + ba_ref[...])
    ig = _sigmoid(jnp.dot(x16, wx_ref[...], preferred_element_type=F32) + bx_ref[...])
    log_a = -LRU_C * r * _softplus(-lam_ref[...])
    a = jnp.exp(log_a)
    bin_ = jnp.sqrt(-jnp.tanh(log_a) * (a * a + 1.0)) * (ig * x2)
    a_sc[...] = a.reshape(Bn, tc, B_WIDTH)
    b_sc[...] = bin_.reshape(Bn, tc, B_WIDTH)

    def step(t, h):
        h = a_sc[:, pl.ds(t, 1), :] * h + b_sc[:, pl.ds(t, 1), :]
        a_sc[:, pl.ds(t, 1), :] = h
        return h

    h = lax.fori_loop(0, nv, step, h_sc[...])
    h_sc[...] = h
    h1_ref[...] = h
    y_ref[...] = _silu(zb_ref[...]) * a_sc[...]


def _lru(u3, conv0, h0, cw, cb, wa_bd, ba, wx_bd, bx, lam, tc, l_valid):
    Bn, Lp, _ = u3.shape
    row = lambda v: v.reshape(1, B_WIDTH)
    const2 = lambda c: (0, 0)
    return pl.pallas_call(
        functools.partial(_lru_kernel, Bn=Bn, tc=tc, l_valid=l_valid),
        grid=(Lp // tc,),
        in_specs=[pl.BlockSpec((Bn, tc, COL), lambda c: (0, c, CB_XB)),
                  pl.BlockSpec((Bn, tc, COL), lambda c: (0, c, CB_ZB)),
                  pl.BlockSpec((Bn, CONV_WIDTH - 1, B_WIDTH), lambda c: (0, 0, 0)),
                  pl.BlockSpec((Bn, 1, B_WIDTH), lambda c: (0, 0, 0)),
                  pl.BlockSpec((CONV_WIDTH, B_WIDTH), const2),
                  pl.BlockSpec((1, B_WIDTH), const2),
                  pl.BlockSpec((B_WIDTH, B_WIDTH), const2),
                  pl.BlockSpec((1, B_WIDTH), const2),
                  pl.BlockSpec((B_WIDTH, B_WIDTH), const2),
                  pl.BlockSpec((1, B_WIDTH), const2),
                  pl.BlockSpec((1, B_WIDTH), const2)],
        out_specs=[pl.BlockSpec((Bn, tc, B_WIDTH), lambda c: (0, c, 0)),
                   pl.BlockSpec((Bn, CONV_WIDTH - 1, B_WIDTH), lambda c: (0, 0, 0)),
                   pl.BlockSpec((Bn, 1, B_WIDTH), lambda c: (0, 0, 0))],
        out_shape=[jax.ShapeDtypeStruct((Bn, Lp, B_WIDTH), F32),
                   jax.ShapeDtypeStruct((Bn, CONV_WIDTH - 1, B_WIDTH), F32),
                   jax.ShapeDtypeStruct((Bn, 1, B_WIDTH), F32)],
        scratch_shapes=[pltpu.VMEM((Bn, SUBLANES + tc, B_WIDTH), F32),
                        pltpu.VMEM((Bn, 1, B_WIDTH), F32),
                        pltpu.VMEM((Bn, tc, B_WIDTH), F32),
                        pltpu.VMEM((Bn, tc, B_WIDTH), F32)],
        compiler_params=_cparams("arbitrary"),
        name="rglru",
    )(u3, u3, conv0, h0.reshape(Bn, 1, B_WIDTH), cw, row(cb), wa_bd, row(ba), wx_bd, row(bx), row(lam))


def _gdn_kernel(qkv_ref, beta_ref, alpha_ref, z_ref, conv0_ref, s0_ref, cw_ref, alog_ref, dtb_ref, nw_ref,
                o_ref, conv1_ref, s1_ref, xp_sc, s_sc, y_sc, *, tb, l_valid):
    C = DELTA_CHUNK
    rows_c = max(tb, C)
    nc = rows_c // C
    t = pl.program_id(1)

    @pl.when(t == 0)
    def _():
        xp_sc[TAIL0:SUBLANES, :] = conv0_ref[...]
        s_sc[...] = s0_ref[...]

    xp_sc[SUBLANES:SUBLANES + tb, :] = qkv_ref[...]
    y = xp_sc[TAIL0:TAIL0 + tb, :] * cw_ref[0:1, :]
    for j in range(1, CONV_WIDTH):
        y = y + xp_sc[TAIL0 + j:TAIL0 + j + tb, :] * cw_ref[j:j + 1, :]
    nv = min(tb, l_valid)
    tail = xp_sc[TAIL0 + nv:SUBLANES + nv, :]
    conv1_ref[...] = tail
    xp_sc[TAIL0:SUBLANES, :] = tail
    y = _silu(y)

    if nv < rows_c:
        row_io = lax.broadcasted_iota(jnp.int32, (tb, 1), 0)
        live = row_io < nv
        y_sc[...] = jnp.zeros_like(y_sc)
        y_sc[0:tb, :] = jnp.where(live, y, 0.0)
    else:
        live = None
        y_sc[...] = y

    ri = lax.broadcasted_iota(jnp.int32, (C, C), 0)
    ci = lax.broadcasted_iota(jnp.int32, (C, C), 1)
    causal = ri >= ci
    strict = ri > ci
    eye = jnp.where(ri == ci, 1.0, 0.0)
    ltri = jnp.where(causal, 1.0, 0.0)

    def padrows(v):
        if tb == rows_c:
            return v
        return jnp.concatenate([v, jnp.zeros((rows_c - tb, v.shape[1]), F32)], axis=0)

    for h in range(C_HEADS):
        hs = slice(h * C_KEY_DIM, (h + 1) * C_KEY_DIM)
        qa = y_sc[:, h * C_KEY_DIM:(h + 1) * C_KEY_DIM]
        ka = y_sc[:, C_QK_WIDTH + h * C_KEY_DIM:C_QK_WIDTH + (h + 1) * C_KEY_DIM]
        va = y_sc[:, 2 * C_QK_WIDTH + h * C_VAL_DIM:2 * C_QK_WIDTH + (h + 1) * C_VAL_DIM]
        qa = qa * lax.rsqrt(jnp.sum(qa * qa, axis=-1, keepdims=True) + EPS) * (C_KEY_DIM ** -0.5)
        ka = ka * lax.rsqrt(jnp.sum(ka * ka, axis=-1, keepdims=True) + EPS)
        beta = _sigmoid(beta_ref[:, hs])
        g = -jnp.exp(alog_ref[:, hs]) * _softplus(alpha_ref[:, hs] + dtb_ref[:, hs])
        if live is not None:
            beta = jnp.where(live, beta, 0.0)
            g = jnp.where(live, g, 0.0)
        beta = padrows(beta)
        g = padrows(g)
        S = s_sc[h]
        for c in range(nc):
            rs = slice(c * C, (c + 1) * C)
            qc, kc, vc, bc, gc = qa[rs], ka[rs], va[rs], beta[rs], g[rs]
            G = jnp.dot(ltri, gc, precision=HIGHEST, preferred_element_type=F32)
            Gcol = G[:, 0:C]
            Grow = jnp.sum(jnp.where(ri == ci, Gcol, 0.0), axis=0, keepdims=True)
            decay = jnp.exp(jnp.where(causal, Gcol - Grow, -jnp.inf))
            kb = kc * bc
            A = jnp.where(strict, lax.dot_general(kb, kc, NT_DIMS, preferred_element_type=F32) * decay, 0.0)
            M = -A
            Tm = eye + M
            for _ in range(5):
                M = jnp.dot(M, M, precision=HIGHEST, preferred_element_type=F32)
                Tm = Tm + jnp.dot(Tm, M, precision=HIGHEST, preferred_element_type=F32)
            eG = jnp.exp(G)
            UW = jnp.dot(Tm, jnp.concatenate([vc * bc, kb * eG], axis=1), preferred_element_type=F32)
            U = UW[:, 0:C_VAL_DIM]
            W = UW[:, C_VAL_DIM:]
            QK = lax.dot_general(qc, kc, NT_DIMS, preferred_element_type=F32) * decay
            qg = qc * eG
            Glast = G[C - 1:C, :]
            kg = kc * jnp.exp(Glast - G)
            gl = jnp.exp(Glast)
            vn = U - jnp.dot(W, S, preferred_element_type=F32)
            o = jnp.dot(qg, S, preferred_element_type=F32) + jnp.dot(QK, vn, preferred_element_type=F32)
            S = S * gl + lax.dot_general(kg, vn, TN_DIMS, preferred_element_type=F32)
            o = o * lax.rsqrt(jnp.mean(o * o, axis=-1, keepdims=True) + EPS) * nw_ref[...]
            if tb >= C:
                o_ref[rs, hs] = o * _silu(z_ref[rs, hs])
            else:
                o_ref[:, hs] = o[0:tb] * _silu(z_ref[:, hs])
        s_sc[h] = S
        s1_ref[h] = S


def _gdn(u3, conv0, s0, cw, alog_e, dtb_e, nw, tb, l_valid):
    Bn, Lp, _ = u3.shape
    rows_c = max(tb, DELTA_CHUNK)
    const2 = lambda b, t: (0, 0)
    return pl.pallas_call(
        functools.partial(_gdn_kernel, tb=tb, l_valid=l_valid),
        grid=(Bn, Lp // tb),
        in_specs=[pl.BlockSpec((None, tb, C_CONV_DIM), lambda b, t: (b, t, CB_QKVC * COL // C_CONV_DIM)),
                  pl.BlockSpec((None, tb, COL), lambda b, t: (b, t, CB_BETA)),
                  pl.BlockSpec((None, tb, COL), lambda b, t: (b, t, CB_ALPHA)),
                  pl.BlockSpec((None, tb, COL), lambda b, t: (b, t, CB_ZC)),
                  pl.BlockSpec((None, CONV_WIDTH - 1, C_CONV_DIM), lambda b, t: (b, 0, 0)),
                  pl.BlockSpec((None, C_HEADS, C_KEY_DIM, C_VAL_DIM), lambda b, t: (b, 0, 0, 0)),
                  pl.BlockSpec((CONV_WIDTH, C_CONV_DIM), const2),
                  pl.BlockSpec((1, C_V_WIDTH), const2),
                  pl.BlockSpec((1, C_V_WIDTH), const2),
                  pl.BlockSpec((1, C_VAL_DIM), const2)],
        out_specs=[pl.BlockSpec((None, tb, C_V_WIDTH), lambda b, t: (b, t, 0)),
                   pl.BlockSpec((None, CONV_WIDTH - 1, C_CONV_DIM), lambda b, t: (b, 0, 0)),
                   pl.BlockSpec((None, C_HEADS, C_KEY_DIM, C_VAL_DIM), lambda b, t: (b, 0, 0, 0))],
        out_shape=[jax.ShapeDtypeStruct((Bn, Lp, C_V_WIDTH), F32),
                   jax.ShapeDtypeStruct((Bn, CONV_WIDTH - 1, C_CONV_DIM), F32),
                   jax.ShapeDtypeStruct((Bn, C_HEADS, C_KEY_DIM, C_VAL_DIM), F32)],
        scratch_shapes=[pltpu.VMEM((SUBLANES + tb, C_CONV_DIM), F32),
                        pltpu.VMEM((C_HEADS, C_KEY_DIM, C_VAL_DIM), F32),
                        pltpu.VMEM((rows_c, C_CONV_DIM), F32)],
        compiler_params=_cparams("parallel", "arbitrary"),
        name="gdn",
    )(u3, u3, u3, u3, conv0, s0, cw, alog_e, dtb_e, nw.reshape(1, C_VAL_DIM))


def _merge_kernel(h_ref, oa_ref, za_ref, ob_ref, oc_ref, g0_ref, g1_ref, g2_ref,
                  wpa_ref, wpb_ref, wpc_ref, wout_ref, fnw_ref, hn_ref, *, final_norm):
    def proj(x, w_ref):
        return jnp.dot(x.astype(BF16), w_ref[...], preferred_element_type=F32)

    ya = proj(_silu(za_ref[...]) * oa_ref[...], wpa_ref)
    yb = proj(ob_ref[...], wpb_ref)
    yc = proj(oc_ref[...], wpc_ref)
    merged = _sigmoid(g0_ref[...]) * ya + _sigmoid(g1_ref[...]) * yb + _sigmoid(g2_ref[...]) * yc
    hn = h_ref[...] + proj(merged, wout_ref)
    if final_norm:
        hn = hn * lax.rsqrt(jnp.mean(hn * hn, axis=-1, keepdims=True) + EPS) * fnw_ref[...]
    hn_ref[...] = hn


def _merge(h2d, oa, u, ob, oc, wpa, wpb, wpc, wout, fnw, tm, final_norm):
    T = h2d.shape[0]
    gpb = D_MODEL // COL
    tok = lambda w: pl.BlockSpec((tm, w), lambda i: (i, 0))
    wspec = lambda k: pl.BlockSpec((k, D_MODEL), lambda i: (0, 0))
    gate_spec = lambda g: pl.BlockSpec((tm, D_MODEL), lambda i: (i, CB_GATES // gpb + g))
    return pl.pallas_call(
        functools.partial(_merge_kernel, final_norm=final_norm),
        grid=(T // tm,),
        in_specs=[tok(D_MODEL), tok(A_WIDTH),
                  pl.BlockSpec((tm, COL), lambda i: (i, CB_ZA)),
                  tok(B_WIDTH), tok(C_V_WIDTH),
                  gate_spec(0), gate_spec(1), gate_spec(2),
                  wspec(A_WIDTH), wspec(B_WIDTH), wspec(C_V_WIDTH), wspec(D_MODEL),
                  pl.BlockSpec((1, D_MODEL), lambda i: (0, 0))],
        out_specs=tok(D_MODEL),
        out_shape=jax.ShapeDtypeStruct((T, D_MODEL), F32),
        compiler_params=_cparams("parallel"),
        name="merge",
    )(h2d, oa, u, ob, oc, u, u, u, wpa, wpb, wpc, wout, fnw.reshape(1, D_MODEL))


def _block_diag(w):
    g, n, _ = w.shape
    out = jnp.zeros((g * n, g * n), w.dtype)
    for i in range(g):
        out = out.at[i * n:(i + 1) * n, i * n:(i + 1) * n].set(w[i])
    return out


def _rearrange_w_in(w):
    n_main = 4 * A_WIDTH + 2 * B_WIDTH + C_CONV_DIM + C_V_WIDTH
    main = w[:, :n_main]
    beta = w[:, n_main:n_main + C_HEADS]
    alpha = w[:, n_main + C_HEADS:n_main + 2 * C_HEADS]
    gates = w[:, n_main + 2 * C_HEADS:]
    return jnp.concatenate([main, gates, jnp.repeat(beta, C_VAL_DIM, axis=1), jnp.repeat(alpha, C_VAL_DIM, axis=1)],
                           axis=1).astype(BF16)


def kernel(x_prompt, x_sample, cache_k, cache_v, page_table, state_conv_b, state_h_b, state_conv_c, state_s_c,
           norm_w, w_in, conv_b_w, conv_b_b, lru_wa, lru_ba, lru_wx, lru_bx, lru_lambda,
           conv_c_w, dn_a_log, dn_dt_bias, dn_norm_w, w_pa, w_pb, w_pc, w_out, final_norm_w):
    Bp, L, _ = x_prompt.shape
    Bs, dec_seq, _ = x_sample.shape
    depth = w_in.shape[0]
    n_pool = cache_k.shape[1]
    n_pages = page_table.shape[1]
    past = n_pages * PAGE_SIZE
    tpad = SUBLANES
    assert dec_seq <= tpad and L % MOBA_BLOCK == 0 and past % MOBA_BLOCK == 0 and n_pages % PAGES_PER_STEP == 0
    assert dec_seq * A_HEADS <= HQ

    slopes = jnp.exp2(-8.0 * jnp.arange(1, A_HEADS + 1, dtype=F32) / A_HEADS)
    lane = jnp.arange(HQ)
    slope_lane = slopes[lane % A_HEADS].reshape(1, HQ)
    tq_lane = (past + lane // A_HEADS).astype(F32).reshape(1, HQ)
    ck = cache_k.reshape(depth, n_pool, PAGE_SIZE, A_WIDTH)
    cv = cache_v.reshape(depth, n_pool, PAGE_SIZE, A_WIDTH)

    hp = x_prompt.reshape(Bp * L, D_MODEL)
    hs = jnp.pad(x_sample, ((0, 0), (0, tpad - dec_seq), (0, 0))).reshape(Bs * tpad, D_MODEL)
    zeros = lambda *s: jnp.zeros(s, F32)
    tm_p = 1024 if (Bp * L) % 1024 == 0 else MOBA_BLOCK
    tm_s = Bs * tpad
    tc_p = 256 if L % 256 == 0 else L
    outs = [[] for _ in range(12)]
    for l in range(depth):
        w_re = _rearrange_w_in(w_in[l])
        wa_bd = _block_diag(lru_wa[l]).astype(BF16)
        wx_bd = _block_diag(lru_wx[l]).astype(BF16)
        alog_e = jnp.repeat(dn_a_log[l], C_VAL_DIM).reshape(1, C_V_WIDTH)
        dtb_e = jnp.repeat(dn_dt_bias[l], C_VAL_DIM).reshape(1, C_V_WIDTH)
        wpa, wpb, wpc, wout = (w.astype(BF16) for w in (w_pa[l], w_pb[l], w_pc[l], w_out[l]))
        last = l == depth - 1
        lru_w = (conv_b_w[l], conv_b_b[l], wa_bd, lru_ba[l], wx_bd, lru_bx[l], lru_lambda[l])
        gdn_w = (conv_c_w[l], alog_e, dtb_e, dn_norm_w[l])

        u = _inproj(hp, norm_w[l], w_re, tm_p)
        u3 = u.reshape(Bp, L, N_U)
        oa = _moba_prompt(u, slopes, Bp, L)
        ob, cbp, hbp = _lru(u3, zeros(Bp, CONV_WIDTH - 1, B_WIDTH), zeros(Bp, B_WIDTH), *lru_w, tc_p, L)
        oc, ccp, scp = _gdn(u3, zeros(Bp, CONV_WIDTH - 1, C_CONV_DIM),
                            zeros(Bp, C_HEADS, C_KEY_DIM, C_VAL_DIM), *gdn_w, tc_p, L)
        hp = _merge(hp, oa, u, ob.reshape(Bp * L, B_WIDTH), oc.reshape(Bp * L, C_V_WIDTH),
                    wpa, wpb, wpc, wout, final_norm_w, tm_p // 2, last)
        kp =u3[:, :, CB_KA * COL:(CB_KA + 1) * COL].reshape(Bp, L, A_HEADS, A_HEAD_DIM)
        vp = u3[:, :, CB_VA * COL:(CB_VA + 1) * COL].reshape(Bp, L, A_HEADS, A_HEAD_DIM)

        us = _inproj(hs, norm_w[l], w_re, tm_s)
        us3 = us.reshape(Bs, tpad, N_U)
        oas = _moba_sample(us, ck, cv, page_table, slope_lane, tq_lane, l, Bs, tpad, dec_seq)
        obs, cbs, hbs = _lru(us3, state_conv_b[l], state_h_b[l], *lru_w, tpad, dec_seq)
        ocs, ccs, scs = _gdn(us3, state_conv_c[l], state_s_c[l], *gdn_w, tpad, dec_seq)
        hs = _merge(hs, oas, us, obs.reshape(Bs * tpad, B_WIDTH), ocs.reshape(Bs * tpad, C_V_WIDTH),
                    wpa, wpb, wpc, wout, final_norm_w, tm_s, last)
        ks = us3[:, :dec_seq, CB_KA * COL:(CB_KA + 1) * COL].reshape(Bs, dec_seq, A_HEADS, A_HEAD_DIM)
        vs = us3[:, :dec_seq, CB_VA * COL:(CB_VA + 1) * COL].reshape(Bs, dec_seq, A_HEADS, A_HEAD_DIM)

        hbp = hbp.reshape(Bp, B_WIDTH)
        hbs = hbs.reshape(Bs, B_WIDTH)
        for lst, val in zip(outs, (kp, vp, ks, vs, cbp, hbp, cbs, hbs, ccp, scp, ccs, scs)):
            lst.append(val)

    y_prompt = hp.reshape(Bp, L, D_MODEL)
    y_sample = hs.reshape(Bs, tpad, D_MODEL)[:, :dec_seq]
    return (y_prompt, y_sample) + tuple(jnp.stack(o) for o in outs)
```

```python
import functools

import jax
import jax.numpy as jnp
from jax import lax
from jax.experimental import pallas as pl
from jax.experimental.pallas import tpu as pltpu

F32 = jnp.float32
BF16 = jnp.bfloat16
HIGHEST = lax.Precision.HIGHEST

D_MODEL = 1024
A_HEADS = 8
A_HEAD_DIM = 64
A_WIDTH = A_HEADS * A_HEAD_DIM
MOBA_BLOCK = 256
MOBA_TOPK = 3
PAGE_SIZE = 128
B_WIDTH = 512
B_BLOCKS = 8
LRU_C = 8.0
CONV_WIDTH = 4
C_HEADS = 4
C_KEY_DIM = 128
C_VAL_DIM = 128
C_QK_WIDTH = C_HEADS * C_KEY_DIM
C_V_WIDTH = C_HEADS * C_VAL_DIM
C_CONV_DIM = 2 * C_QK_WIDTH + C_V_WIDTH
DELTA_CHUNK = 64
N_BRANCH = 3
EPS = 1e-6

COL = 512
CB_QA, CB_KA, CB_VA, CB_ZA, CB_XB, CB_ZB, CB_QKVC, CB_ZC, CB_GATES, CB_BETA, CB_ALPHA = 0, 1, 2, 3, 4, 5, 6, 9, 10, 16, 17
N_COLBLK = 18
N_U = N_COLBLK * COL

SUBLANES = 8
LANES = 128
BF16_ROWS = 16
TAIL0 = SUBLANES - (CONV_WIDTH - 1)
VMEM_LIMIT = 56 * 1024 * 1024

NT_DIMS = (((1,), (1,)), ((), ()))
TN_DIMS = (((0,), (0,)), ((), ()))


def _cparams(*sem):
    return pltpu.CompilerParams(dimension_semantics=sem, vmem_limit_bytes=VMEM_LIMIT)


def _sigmoid(x):
    return 1.0 / (1.0 + jnp.exp(-x))


def _silu(x):
    return x * _sigmoid(x)


def _softplus(x):
    return jnp.maximum(x, 0.0) + jnp.log1p(jnp.exp(-jnp.abs(x)))


def _mm(a, b):
    return jnp.dot(a, b, preferred_element_type=F32)


def _mm_nt(a, b):
    return lax.dot_general(a, b, NT_DIMS, preferred_element_type=F32)


def _split2(x):
    hi = x.astype(BF16)
    lo = (x - hi.astype(F32)).astype(BF16)
    return hi, lo


def _mm_split(a, b):
    (ah, al), (bh, bl) = a, b
    return _mm(ah, bh) + (_mm(ah, bl) + _mm(al, bh))


def _inproj_kernel(h_ref, nw_ref, w_ref, u_ref, xn_sc):
    @pl.when(pl.program_id(1) == 0)
    def _():
        x = h_ref[...]
        y = x * lax.rsqrt(jnp.mean(x * x, axis=-1, keepdims=True) + EPS) * nw_ref[...]
        xn_sc[...] = y.astype(BF16)

    u_ref[...] = jnp.dot(xn_sc[...], w_ref[...], preferred_element_type=F32)


def _inproj(h2d, norm_w, w_re, tm):
    T = h2d.shape[0]
    return pl.pallas_call(
        _inproj_kernel,
        grid=(T // tm, N_COLBLK),
        in_specs=[pl.BlockSpec((tm, D_MODEL), lambda i, j: (i, 0)),
                  pl.BlockSpec((1, D_MODEL), lambda i, j: (0, 0)),
                  pl.BlockSpec((D_MODEL, COL), lambda i, j: (0, j))],
        out_specs=pl.BlockSpec((tm, COL), lambda i, j: (i, j)),
        out_shape=jax.ShapeDtypeStruct((T, N_U), F32),
        scratch_shapes=[pltpu.VMEM((tm, D_MODEL), BF16)],
        compiler_params=_cparams("parallel", "arbitrary"),
        name="inproj",
    )(h2d, norm_w.reshape(1, D_MODEL), w_re)


def _topk_select(gate, n_past, n_iota):
    nb = gate.shape[0]
    g = jnp.where(n_iota < n_past, gate, -jnp.inf)
    rank = jnp.zeros_like(g)
    for m in range(nb):
        gm = g[m:m + 1, :]
        rank = rank + jnp.where(n_iota > m, jnp.where(gm >= g, 1.0, 0.0), jnp.where(gm > g, 1.0, 0.0))
    return jnp.where(n_iota < n_past, jnp.where(rank < MOBA_TOPK, 1.0, 0.0), 0.0)


def _moba_prompt_kernel(slopes_ref, q_ref, k_ref, v_ref, o_ref, kb_sc, vt_sc, means_sc, cmask_sc, sel_sc, *, nblk):
    BLK, HD, PW = MOBA_BLOCK, A_HEAD_DIM, 2 * A_HEAD_DIM
    pair = pl.program_id(1)
    i = pl.program_id(2)
    lane = lax.broadcasted_iota(jnp.int32, (1, PW), 1)

    @pl.when(i == 0)
    def _init():
        c_col = lax.broadcasted_iota(jnp.int32, (BLK, PW), 0).astype(F32)
        lane_a = lax.broadcasted_iota(jnp.int32, (BLK, PW), 1)
        key_off = jnp.where(lane_a == 0, c_col, 0.0).astype(BF16)
        ones = jnp.ones((BF16_ROWS, BLK), BF16)
        for n in range(nblk):
            kblk = k_ref[n * BLK:(n + 1) * BLK, :]
            mean = jnp.sum(kblk, axis=0, keepdims=True) * (1.0 / BLK)
            means_sc[n:n + 1, :] = jnp.where(lane < HD, mean, 0.0)
            means_sc[nblk + n:nblk + n + 1, :] = jnp.where(lane < HD, 0.0, mean)
            kb_sc[n, :, 0:PW] = kblk.astype(BF16)
            kb_sc[n, :, PW:2 * PW] = key_off
            vt = v_ref[n * BLK:(n + 1) * BLK, :].T
            for hh in range(2):
                vt_sc[n, hh, 0:HD, :] = vt[hh * HD:(hh + 1) * HD, :].astype(BF16)
                vt_sc[n, hh, HD:HD + BF16_ROWS, :] = ones
        c_io = lax.broadcasted_iota(jnp.int32, (BLK, BLK), 0)
        r_io = lax.broadcasted_iota(jnp.int32, (BLK, BLK), 1)
        cmask_sc[...] = jnp.where(c_io <= r_io, 0.0, -jnp.inf)

    qf = q_ref[...] * (HD ** -0.5)
    gate2 = lax.dot_general(means_sc[...], qf, NT_DIMS, precision=HIGHEST, preferred_element_type=F32)
    n_iota = lax.broadcasted_iota(jnp.int32, (nblk, BLK), 0)
    lane_q = lax.broadcasted_iota(jnp.int32, (BLK, PW), 1)
    qaugs, slopes = [], []
    for hh in range(2):
        slope = slopes_ref[pair * 2 + hh]
        sel = _topk_select(gate2[hh * nblk:(hh + 1) * nblk, :], i, n_iota)
        for n in range(nblk):
            sel_sc[hh, n] = sel[n:n + 1, :]
        qh = jnp.where((lane_q // HD) == hh, qf, 0.0).astype(BF16)
        slope_lane = jnp.where(lane_q == 0, slope, 0.0).astype(BF16)
        qaugs.append(jnp.concatenate([qh, slope_lane], axis=1))
        slopes.append(slope)

    def colmax8(s):
        out = s[0:SUBLANES]
        for r in range(1, BLK // SUBLANES):
            out = jnp.maximum(out, s[r * SUBLANES:(r + 1) * SUBLANES])
        return out

    def past_tile(t, hh):
        j = jnp.minimum(t, i - 1)
        s = _mm_nt(kb_sc[j], qaugs[hh])
        on = sel_sc[hh, j] * jnp.where(t < i, 1.0, 0.0) > 0.0
        delta = -slopes[hh] * ((i - j) * BLK).astype(F32)
        return j, s, on, delta

    s_own = [_mm_nt(kb_sc[i], qaugs[hh]) + cmask_sc[...] for hh in range(2)]

    def max_trip(tt, mx):
        mx = list(mx)
        for e in range(2):
            for hh in range(2):
                _, s, on, delta = past_tile(2 * tt + e, hh)
                mx[hh] = jnp.maximum(mx[hh], jnp.where(on, colmax8(s) + delta, -jnp.inf))
        return tuple(mx)

    mx = lax.fori_loop(0, (i + 1) // 2, max_trip, tuple(colmax8(s) for s in s_own))
    m = [jnp.max(x, axis=0, keepdims=True) for x in mx]

    def pv_trip(tt, acc):
        acc = list(acc)
        for e in range(2):
            for hh in range(2):
                j, s, on, delta = past_tile(2 * tt + e, hh)
                p = jnp.exp(s - jnp.where(on, m[hh] - delta, jnp.inf))
                acc[hh] = acc[hh] + _mm(vt_sc[j, hh], p.astype(BF16))
        return tuple(acc)

    acc0 = tuple(_mm(vt_sc[i, hh], jnp.exp(s_own[hh] - m[hh]).astype(BF16)) for hh in range(2))
    acc = lax.fori_loop(0, (i + 1) // 2, pv_trip, acc0)
    outs = [a[0:HD, :] / a[HD:HD + 1, :] for a in acc]
    o_ref[...] = jnp.concatenate(outs, axis=0).T


def _moba_prompt(u, slopes, Bn, L):
    nblk = L // MOBA_BLOCK
    PW = 2 * A_HEAD_DIM
    cpb = COL // PW
    return pl.pallas_call(
        functools.partial(_moba_prompt_kernel, nblk=nblk),
        grid=(Bn, A_HEADS // 2, nblk),
        in_specs=[pl.BlockSpec(memory_space=pltpu.SMEM),
                  pl.BlockSpec((MOBA_BLOCK, PW), lambda b, p, i: (b * nblk + i, CB_QA * cpb + p)),
                  pl.BlockSpec((L, PW), lambda b, p, i: (b, CB_KA * cpb + p)),
                  pl.BlockSpec((L, PW), lambda b, p, i: (b, CB_VA * cpb + p))],
        out_specs=pl.BlockSpec((MOBA_BLOCK, PW), lambda b, p, i: (b * nblk + i, p)),
        out_shape=jax.ShapeDtypeStruct((Bn * L, A_WIDTH), F32),
        scratch_shapes=[pltpu.VMEM((nblk, MOBA_BLOCK, 2 * PW), BF16),
                        pltpu.VMEM((nblk, 2, A_HEAD_DIM + BF16_ROWS, MOBA_BLOCK), BF16),
                        pltpu.VMEM((2 * nblk, PW), F32),
                        pltpu.VMEM((MOBA_BLOCK, MOBA_BLOCK), F32),
                        pltpu.VMEM((2, nblk, 1, MOBA_BLOCK), F32)],
        compiler_params=_cparams("parallel", "parallel", "arbitrary"),
        name="moba_prompt",
    )(slopes, u, u, u)


PAGES_PER_STEP = 16


def _moba_sample_kernel(pt_ref, slope_ref, tq_ref, q_ref, kn_ref, vn_ref, *refs, n_pages, dec_seq):
    del pt_ref
    PPS = PAGES_PER_STEP
    k_refs = refs[:PPS]
    v_refs = refs[PPS:2 * PPS]
    o_ref = refs[2 * PPS]
    qbd_sc, s_sc, p_sc, acc_sc, linv_sc = refs[2 * PPS + 1:]
    ph = pl.program_id(1)
    st = pl.program_id(2)
    nsteps = n_pages // PPS
    ppb = MOBA_BLOCK // PAGE_SIZE
    nb = n_pages // ppb
    NR = dec_seq * A_HEADS
    head_of_row = lax.broadcasted_iota(jnp.int32, (A_HEADS, A_WIDTH), 0)
    head_of_lane = lax.broadcasted_iota(jnp.int32, (A_HEADS, A_WIDTH), 1) // A_HEAD_DIM
    hmask = jnp.where(head_of_row == head_of_lane, 1.0, 0.0)

    @pl.when((ph == 0) & (st == 0))
    def _init():
        for t in range(dec_seq):
            qrow = q_ref[t:t + 1, :] * (A_HEAD_DIM ** -0.5)
            qbd_sc[t * A_HEADS:(t + 1) * A_HEADS, :] = qrow * hmask

    @pl.when(ph == 0)
    def _scores():
        qbd = qbd_sc[...].astype(BF16)
        for r in range(PPS):
            s_sc[st * PPS + r] = _mm(qbd, k_refs[r][...].astype(BF16))

    @pl.when((ph == 0) & (st == nsteps - 1))
    def _softmax():
        slope = slope_ref[...]
        tq = tq_ref[...]
        lane = lax.broadcasted_iota(jnp.int32, (NR, PAGE_SIZE), 1)
        lane_f = lane.astype(F32)
        q_of_row = lax.broadcasted_iota(jnp.int32, (NR, PAGE_SIZE), 0) // A_HEADS
        gate = jnp.full((NR, PAGE_SIZE), -jnp.inf, F32)
        for n in range(nb):
            tot = s_sc[ppb * n]
            for e in range(1, ppb):
                tot = tot + s_sc[ppb * n + e]
            gate = jnp.where(lane == n, jnp.sum(tot, axis=-1, keepdims=True) * (1.0 / MOBA_BLOCK), gate)
        sel = jnp.zeros((NR, PAGE_SIZE), F32)
        for _ in range(MOBA_TOPK):
            mx = jnp.max(gate, axis=-1, keepdims=True)
            first = jnp.min(jnp.where(gate == mx, lane_f, float(PAGE_SIZE)), axis=-1, keepdims=True)
            pick = lane_f == first
            sel = jnp.where(pick, 1.0, sel)
            gate = jnp.where(pick, -jnp.inf, gate)
        pad = jnp.zeros((PAGE_SIZE - kn_ref.shape[0], A_WIDTH), F32)
        qbd = qbd_sc[...].astype(BF16)
        s_own = _mm_nt(qbd, jnp.concatenate([kn_ref[...], pad], axis=0).astype(BF16))
        own_ok = (lane <= q_of_row) & (lane < dec_seq)
        s_own = jnp.where(own_ok, s_own - slope * (q_of_row - lane).astype(F32), -jnp.inf)
        m = jnp.max(s_own, axis=-1, keepdims=True)
        for n in range(nb):
            on = jnp.sum(jnp.where(lane == n, sel, 0.0), axis=-1, keepdims=True) > 0.0
            for e in range(ppb):
                pg = ppb * n + e
                sb = s_sc[pg] - slope * (tq - (lane_f + float(pg * PAGE_SIZE)))
                sb = jnp.where(on, sb, -jnp.inf)
                s_sc[pg] = sb
                m = jnp.maximum(m, jnp.max(sb, axis=-1, keepdims=True))
        p_own = jnp.exp(s_own - m)
        l = jnp.sum(p_own, axis=-1, keepdims=True)
        for pg in range(n_pages):
            pb = jnp.exp(s_sc[pg] - m)
            l = l + jnp.sum(pb, axis=-1, keepdims=True)
            p_sc[pg] = pb.astype(BF16)
        linv_sc[...] = jnp.broadcast_to(1.0 / l, (NR, LANES))
        v_own = jnp.concatenate([vn_ref[...], pad], axis=0).astype(BF16)
        acc_sc[...] = _mm(p_own.astype(BF16), v_own)

    @pl.when(ph == 1)
    def _values():
        acc = acc_sc[...]
        for r in range(PPS):
            acc = acc + _mm_nt(p_sc[st * PPS + r], v_refs[r][...].astype(BF16))
        acc_sc[...] = acc

    @pl.when((ph == 1) & (st == nsteps - 1))
    def _finish():
        linv = jnp.concatenate([linv_sc[...]] * (A_WIDTH // LANES), axis=1)
        a = acc_sc[...] * linv
        o_ref[...] = jnp.zeros_like(o_ref)
        for t in range(dec_seq):
            o_ref[t:t + 1, :] = jnp.sum(a[t * A_HEADS:(t + 1) * A_HEADS, :] * hmask, axis=0, keepdims=True)


def _moba_sample(u_s, ck_t, cv_t, page_table, slope_rows, tq_rows, layer, Bs, tpad, dec_seq):
    n_pages = page_table.shape[1]
    PPS = PAGES_PER_STEP
    nsteps = n_pages // PPS
    NR = dec_seq * A_HEADS
    u3 = u_s.reshape(Bs, tpad, N_U)

    def kmap(r):
        return lambda b, ph, st, pt: (layer, pt[b, (st * (1 - ph) + (nsteps - 1) * ph) * PPS + r], 0, 0)

    def vmap_(r):
        return lambda b, ph, st, pt: (layer, pt[b, (st * ph) * PPS + r], 0, 0)

    page_blk = (None, None, A_WIDTH, PAGE_SIZE)
    in_specs = [pl.BlockSpec((NR, LANES), lambda b, ph, st, pt: (0, 0)),
                pl.BlockSpec((NR, LANES), lambda b, ph, st, pt: (0, 0)),
                pl.BlockSpec((None, tpad, COL), lambda b, ph, st, pt: (b, 0, CB_QA)),
                pl.BlockSpec((None, tpad, COL), lambda b, ph, st, pt: (b, 0, CB_KA)),
                pl.BlockSpec((None, tpad, COL), lambda b, ph, st, pt: (b, 0, CB_VA))]
    in_specs += [pl.BlockSpec(page_blk, kmap(r)) for r in range(PPS)]
    in_specs += [pl.BlockSpec(page_blk, vmap_(r)) for r in range(PPS)]
    grid_spec = pltpu.PrefetchScalarGridSpec(
        num_scalar_prefetch=1,
        grid=(Bs, 2, nsteps),
        in_specs=in_specs,
        out_specs=pl.BlockSpec((None, tpad, A_WIDTH), lambda b, ph, st, pt: (b, 0, 0)),
        scratch_shapes=[pltpu.VMEM((NR, A_WIDTH), F32),
                        pltpu.VMEM((n_pages, NR, PAGE_SIZE), F32),
                        pltpu.VMEM((n_pages, NR, PAGE_SIZE), BF16),
                        pltpu.VMEM((NR, A_WIDTH), F32),
                        pltpu.VMEM((NR, LANES), F32)])
    out = pl.pallas_call(
        functools.partial(_moba_sample_kernel, n_pages=n_pages, dec_seq=dec_seq),
        grid_spec=grid_spec,
        out_shape=jax.ShapeDtypeStruct((Bs, tpad, A_WIDTH), F32),
        compiler_params=_cparams("parallel", "arbitrary", "arbitrary"),
        name="moba_sample",
    )(page_table, slope_rows, tq_rows, u3, u3, u3, *([ck_t] * PPS), *([cv_t] * PPS))
    return out.reshape(Bs * tpad, A_WIDTH)


def _lru_kernel(xb_ref, zb_ref, conv0_ref, h0_ref, cw_ref, cb_ref, wa_ref, ba_ref, wx_ref, bx_ref, lam_ref,
                y_ref, conv1_ref, h1_ref, xp_sc, h_sc, a_sc, b_sc, *, Bn, tc, l_valid):
    c = pl.program_id(0)

    @pl.when(c == 0)
    def _():
        xp_sc[:, TAIL0:SUBLANES, :] = conv0_ref[...]
        h_sc[...] = h0_ref[...]

    xp_sc[:, SUBLANES:SUBLANES + tc, :] = xb_ref[...]
    xc = xp_sc[:, TAIL0:TAIL0 + tc, :] * cw_ref[0:1, :]
    for j in range(1, CONV_WIDTH):
        xc = xc + xp_sc[:, TAIL0 + j:TAIL0 + j + tc, :] * cw_ref[j:j + 1, :]
    xc = xc + cb_ref[...]
    nv = min(tc, l_valid)
    tail = xp_sc[:, TAIL0 + nv:SUBLANES + nv, :]
    conv1_ref[...] = tail
    xp_sc[:, TAIL0:SUBLANES, :] = tail

    x2 = xc.reshape(Bn * tc, B_WIDTH)
    x16 = x2.astype(BF16)
    r = _sigmoid(jnp.dot(x16, wa_ref[...], preferred_element_type=F32) + ba_ref[...])
    ig = _sigmoid(jnp.dot(x16, wx_ref[...], preferred_element_type=F32) + bx_ref[...])
    log_a = -LRU_C * r * _softplus(-lam_ref[...])
    a = jnp.exp(log_a)
    bin_ = jnp.sqrt(-jnp.tanh(log_a) * (a * a + 1.0)) * (ig * x2)
    a_sc[...] = a.reshape(Bn, tc, B_WIDTH)
    b_sc[...] = bin_.reshape(Bn, tc, B_WIDTH)

    def step(t, h):
        h = a_sc[:, pl.ds(t, 1), :] * h + b_sc[:, pl.ds(t, 1), :]
        a_sc[:, pl.ds(t, 1), :] = h
        return h

    h = lax.fori_loop(0, nv, step, h_sc[...])
    h_sc[...] = h
    h1_ref[...] = h
    y_ref[...] = _silu(zb_ref[...]) * a_sc[...]


def _lru(u3, conv0, h0, cw, cb, wa_bd, ba, wx_bd, bx, lam, tc, l_valid):
    Bn, Lp, _ = u3.shape
    row = lambda v: v.reshape(1, B_WIDTH)
    const2 = lambda c: (0, 0)
    return pl.pallas_call(
        functools.partial(_lru_kernel, Bn=Bn, tc=tc, l_valid=l_valid),
        grid=(Lp // tc,),
        in_specs=[pl.BlockSpec((Bn, tc, COL), lambda c: (0, c, CB_XB)),
                  pl.BlockSpec((Bn, tc, COL), lambda c: (0, c, CB_ZB)),
                  pl.BlockSpec((Bn, CONV_WIDTH - 1, B_WIDTH), lambda c: (0, 0, 0)),
                  pl.BlockSpec((Bn, 1, B_WIDTH), lambda c: (0, 0, 0)),
                  pl.BlockSpec((CONV_WIDTH, B_WIDTH), const2),
                  pl.BlockSpec((1, B_WIDTH), const2),
                  pl.BlockSpec((B_WIDTH, B_WIDTH), const2),
                  pl.BlockSpec((1, B_WIDTH), const2),
                  pl.BlockSpec((B_WIDTH, B_WIDTH), const2),
                  pl.BlockSpec((1, B_WIDTH), const2),
                  pl.BlockSpec((1, B_WIDTH), const2)],
        out_specs=[pl.BlockSpec((Bn, tc, B_WIDTH), lambda c: (0, c, 0)),
                   pl.BlockSpec((Bn, CONV_WIDTH - 1, B_WIDTH), lambda c: (0, 0, 0)),
                   pl.BlockSpec((Bn, 1, B_WIDTH), lambda c: (0, 0, 0))],
        out_shape=[jax.ShapeDtypeStruct((Bn, Lp, B_WIDTH), F32),
                   jax.ShapeDtypeStruct((Bn, CONV_WIDTH - 1, B_WIDTH), F32),
                   jax.ShapeDtypeStruct((Bn, 1, B_WIDTH), F32)],
        scratch_shapes=[pltpu.VMEM((Bn, SUBLANES + tc, B_WIDTH), F32),
                        pltpu.VMEM((Bn, 1, B_WIDTH), F32),
                        pltpu.VMEM((Bn, tc, B_WIDTH), F32),
                        pltpu.VMEM((Bn, tc, B_WIDTH), F32)],
        compiler_params=_cparams("arbitrary"),
        name="rglru",
    )(u3, u3, conv0, h0.reshape(Bn, 1, B_WIDTH), cw, row(cb), wa_bd, row(ba), wx_bd, row(bx), row(lam))


def _gdn_kernel(qkv_ref, beta_ref, alpha_ref, z_ref, conv0_ref, s0_ref, cw_ref, alog_ref, dtb_ref, nw_ref,
                o_ref, conv1_ref, s1_ref, xp_sc, s_sc, y_sc, *, tb, l_valid):
    C = DELTA_CHUNK
    R = max(tb, C)
    nc = R // C
    t = pl.program_id(1)

    @pl.when(t == 0)
    def _():
        xp_sc[TAIL0:SUBLANES, :] = conv0_ref[...]
        s_sc[...] = s0_ref[...]

    xp_sc[SUBLANES:SUBLANES + tb, :] = qkv_ref[...]
    y = xp_sc[TAIL0:TAIL0 + tb, :] * cw_ref[0:1, :]
    for j in range(1, CONV_WIDTH):
        y = y + xp_sc[TAIL0 + j:TAIL0 + j + tb, :] * cw_ref[j:j + 1, :]
    nv = min(tb, l_valid)
    tail = xp_sc[TAIL0 + nv:SUBLANES + nv, :]
    conv1_ref[...] = tail
    xp_sc[TAIL0:SUBLANES, :] = tail
    y = _silu(y)

    if nv < R:
        live = lax.broadcasted_iota(jnp.int32, (tb, 1), 0) < nv
        y_sc[...] = jnp.zeros_like(y_sc)
        y_sc[0:tb, :] = jnp.where(live, y, 0.0)
    else:
        live = None
        y_sc[...] = y

    ri = lax.broadcasted_iota(jnp.int32, (R, R), 0)
    ci = lax.broadcasted_iota(jnp.int32, (R, R), 1)
    same = (ri // C) == (ci // C)
    causal = same & (ri >= ci)
    strict = same & (ri > ci)
    diag = ri == ci
    eye = jnp.where(diag, 1.0, 0.0)
    ltri16 = jnp.where(causal, 1.0, 0.0).astype(BF16)

    def padrows(v):
        if tb == R:
            return v
        return jnp.concatenate([v, jnp.zeros((R - tb, v.shape[1]), F32)], axis=0)

    def widen(v):
        return v[:, 0:R] if R <= LANES else jnp.concatenate([v] * (R // LANES), axis=1)

    heads = range(C_HEADS)
    hsl = [slice(h * C_KEY_DIM, (h + 1) * C_KEY_DIM) for h in heads]
    st = []
    for h in heads:
        q = y_sc[:, h * C_KEY_DIM:(h + 1) * C_KEY_DIM]
        k = y_sc[:, C_QK_WIDTH + h * C_KEY_DIM:C_QK_WIDTH + (h + 1) * C_KEY_DIM]
        v = y_sc[:, 2 * C_QK_WIDTH + h * C_VAL_DIM:2 * C_QK_WIDTH + (h + 1) * C_VAL_DIM]
        q = q * lax.rsqrt(jnp.sum(q * q, axis=-1, keepdims=True) + EPS) * (C_KEY_DIM ** -0.5)
        k = k * lax.rsqrt(jnp.sum(k * k, axis=-1, keepdims=True) + EPS)
        beta = _sigmoid(beta_ref[:, hsl[h]])
        g = -jnp.exp(alog_ref[:, hsl[h]]) * _softplus(alpha_ref[:, hsl[h]] + dtb_ref[:, hsl[h]])
        if live is not None:
            beta = jnp.where(live, beta, 0.0)
            g = jnp.where(live, g, 0.0)
        beta = padrows(beta)
        g = padrows(g)
        g_hi, g_lo = _split2(g)
        g_lo2 = (g - g_hi.astype(F32) - g_lo.astype(F32)).astype(BF16)
        G = _mm(ltri16, g_hi) + (_mm(ltri16, g_lo) + _mm(ltri16, g_lo2))
        Gcol = widen(G)
        Grow = jnp.sum(jnp.where(diag, Gcol, 0.0), axis=0, keepdims=True)
        decay = jnp.exp(jnp.where(causal, Gcol - Grow, -jnp.inf))
        kb = k * beta
        k16 = k.astype(BF16)
        A = jnp.where(strict, _mm_nt(kb.astype(BF16), k16) * decay, 0.0)
        eG = jnp.exp(G)
        rhs16 = jnp.concatenate([v * beta, kb * eG], axis=1).astype(BF16)
        QK16 = (_mm_nt(q.astype(BF16), k16) * decay).astype(BF16)
        glast = jnp.concatenate([jnp.broadcast_to(G[(c + 1) * C - 1:(c + 1) * C, :], (C, LANES)) for c in range(nc)],
                                axis=0)
        st.append(dict(M=-A, rhs16=rhs16, QK16=QK16, qg16=(q * eG).astype(BF16),
                       kg16=(k * jnp.exp(glast - G)).astype(BF16),
                       gl=[jnp.exp(G[(c + 1) * C - 1:(c + 1) * C, :]) for c in range(nc)]))

    for s in st:
        s["T"] = eye + s["M"]
        s["Ms"] = _split2(s["M"])
    for _ in range(5):
        for s in st:
            s["M"] = _mm_split(s["Ms"], s["Ms"])
        for s in st:
            s["Ms"] = _split2(s["M"])
            s["T"] = s["T"] + _mm_split(_split2(s["T"]), s["Ms"])
    for h, s in zip(heads, st):
        UW = _mm(s["T"].astype(BF16), s["rhs16"])
        s["U"] = UW[:, 0:C_VAL_DIM]
        s["W16"] = UW[:, C_VAL_DIM:].astype(BF16)
        s["S"] = s_sc[h]
        s["vn"], s["oi"] = [], []

    for c in range(nc):
        rs = slice(c * C, (c + 1) * C)
        for s in st:
            S16 = s["S"].astype(BF16)
            vn = s["U"][rs] - _mm(s["W16"][rs], S16)
            s["oi"].append(_mm(s["qg16"][rs], S16))
            s["S"] = s["S"] * s["gl"][c] + lax.dot_general(s["kg16"][rs], vn.astype(BF16), TN_DIMS,
                                                           preferred_element_type=F32)
            s["vn"].append(vn)
    for h, s in zip(heads, st):
        s_sc[h] = s["S"]
        s1_ref[h] = s["S"]
        o = jnp.concatenate(s["oi"], axis=0) + _mm(s["QK16"], jnp.concatenate(s["vn"], axis=0).astype(BF16))
        o = o * lax.rsqrt(jnp.mean(o * o, axis=-1, keepdims=True) + EPS) * nw_ref[...]
        o_ref[:, hsl[h]] = o[0:tb] * _silu(z_ref[:, hsl[h]])


def _gdn(u3, conv0, s0, cw, alog_e, dtb_e, nw, tb, l_valid):
    Bn, Lp, _ = u3.shape
    rows = max(tb, DELTA_CHUNK)
    const2 = lambda b, t: (0, 0)
    return pl.pallas_call(
        functools.partial(_gdn_kernel, tb=tb, l_valid=l_valid),
        grid=(Bn, Lp // tb),
        in_specs=[pl.BlockSpec((None, tb, C_CONV_DIM), lambda b, t: (b, t, CB_QKVC * COL // C_CONV_DIM)),
                  pl.BlockSpec((None, tb, COL), lambda b, t: (b, t, CB_BETA)),
                  pl.BlockSpec((None, tb, COL), lambda b, t: (b, t, CB_ALPHA)),
                  pl.BlockSpec((None, tb, COL), lambda b, t: (b, t, CB_ZC)),
                  pl.BlockSpec((None, CONV_WIDTH - 1, C_CONV_DIM), lambda b, t: (b, 0, 0)),
                  pl.BlockSpec((None, C_HEADS, C_KEY_DIM, C_VAL_DIM), lambda b, t: (b, 0, 0, 0)),
                  pl.BlockSpec((CONV_WIDTH, C_CONV_DIM), const2),
                  pl.BlockSpec((1, C_V_WIDTH), const2),
                  pl.BlockSpec((1, C_V_WIDTH), const2),
                  pl.BlockSpec((1, C_VAL_DIM), const2)],
        out_specs=[pl.BlockSpec((None, tb, C_V_WIDTH), lambda b, t: (b, t, 0)),
                   pl.BlockSpec((None, CONV_WIDTH - 1, C_CONV_DIM), lambda b, t: (b, 0, 0)),
                   pl.BlockSpec((None, C_HEADS, C_KEY_DIM, C_VAL_DIM), lambda b, t: (b, 0, 0, 0))],
        out_shape=[jax.ShapeDtypeStruct((Bn, Lp, C_V_WIDTH), F32),
                   jax.ShapeDtypeStruct((Bn, CONV_WIDTH - 1, C_CONV_DIM), F32),
                   jax.ShapeDtypeStruct((Bn, C_HEADS, C_KEY_DIM, C_VAL_DIM), F32)],
        scratch_shapes=[pltpu.VMEM((SUBLANES + tb, C_CONV_DIM), F32),
                        pltpu.VMEM((C_HEADS, C_KEY_DIM, C_VAL_DIM), F32),
                        pltpu.VMEM((rows, C_CONV_DIM), F32)],
        compiler_params=_cparams("parallel", "arbitrary"),
        name="gdn",
    )(u3, u3, u3, u3, conv0, s0, cw, alog_e, dtb_e, nw.reshape(1, C_VAL_DIM))


def _merge_kernel(h_ref, oa_ref, za_ref, ob_ref, oc_ref, g0_ref, g1_ref, g2_ref,
                  wpa_ref, wpb_ref, wpc_ref, wout_ref, fnw_ref, hn_ref, *, final_norm):
    def proj(x, w_ref):
        return jnp.dot(x.astype(BF16), w_ref[...], preferred_element_type=F32)

    ya = proj(_silu(za_ref[...]) * oa_ref[...], wpa_ref)
    yb = proj(ob_ref[...], wpb_ref)
    yc = proj(oc_ref[...], wpc_ref)
    merged = _sigmoid(g0_ref[...]) * ya + _sigmoid(g1_ref[...]) * yb + _sigmoid(g2_ref[...]) * yc
    hn = h_ref[...] + proj(merged, wout_ref)
    if final_norm:
        hn = hn * lax.rsqrt(jnp.mean(hn * hn, axis=-1, keepdims=True) + EPS) * fnw_ref[...]
    hn_ref[...] = hn


def _merge(h2d, oa, u, ob, oc, wpa, wpb, wpc, wout, fnw, tm, final_norm):
    T = h2d.shape[0]
    gpb = D_MODEL // COL
    tok = lambda w: pl.BlockSpec((tm, w), lambda i: (i, 0))
    wspec = lambda k: pl.BlockSpec((k, D_MODEL), lambda i: (0, 0))
    gate_spec = lambda g: pl.BlockSpec((tm, D_MODEL), lambda i: (i, CB_GATES // gpb + g))
    return pl.pallas_call(
        functools.partial(_merge_kernel, final_norm=final_norm),
        grid=(T // tm,),
        in_specs=[tok(D_MODEL), tok(A_WIDTH),
                  pl.BlockSpec((tm, COL), lambda i: (i, CB_ZA)),
                  tok(B_WIDTH), tok(C_V_WIDTH),
                  gate_spec(0), gate_spec(1), gate_spec(2),
                  wspec(A_WIDTH), wspec(B_WIDTH), wspec(C_V_WIDTH), wspec(D_MODEL),
                  pl.BlockSpec((1, D_MODEL), lambda i: (0, 0))],
        out_specs=tok(D_MODEL),
        out_shape=jax.ShapeDtypeStruct((T, D_MODEL), F32),
        compiler_params=_cparams("parallel"),
        name="merge",
    )(h2d, oa, u, ob, oc, u, u, u, wpa, wpb, wpc, wout, fnw.reshape(1, D_MODEL))


def _block_diag(w):
    g, n, _ = w.shape
    out = jnp.zeros((g * n, g * n), w.dtype)
    for i in range(g):
        out = out.at[i * n:(i + 1) * n, i * n:(i + 1) * n].set(w[i])
    return out


def _rearrange_w_in(w):
    n_main = 4 * A_WIDTH + 2 * B_WIDTH + C_CONV_DIM + C_V_WIDTH
    main = w[:, :n_main]
    beta = w[:, n_main:n_main + C_HEADS]
    alpha = w[:, n_main + C_HEADS:n_main + 2 * C_HEADS]
    gates = w[:, n_main + 2 * C_HEADS:]
    return jnp.concatenate([main, gates, jnp.repeat(beta, C_VAL_DIM, axis=1), jnp.repeat(alpha, C_VAL_DIM, axis=1)],
                           axis=1).astype(BF16)


def kernel(x_prompt, x_sample, cache_k, cache_v, page_table, state_conv_b, state_h_b, state_conv_c, state_s_c,
           norm_w, w_in, conv_b_w, conv_b_b, lru_wa, lru_ba, lru_wx, lru_bx, lru_lambda,
           conv_c_w, dn_a_log, dn_dt_bias, dn_norm_w, w_pa, w_pb, w_pc, w_out, final_norm_w):
    Bp, L, _ = x_prompt.shape
    Bs, dec_seq, _ = x_sample.shape
    depth = w_in.shape[0]
    n_pool = cache_k.shape[1]
    n_pages = page_table.shape[1]
    past = n_pages * PAGE_SIZE
    tpad = SUBLANES
    assert dec_seq <= tpad and L % MOBA_BLOCK == 0 and past % MOBA_BLOCK == 0 and n_pages % PAGES_PER_STEP == 0
    assert MOBA_TOPK <= past // MOBA_BLOCK <= PAGE_SIZE and (dec_seq * A_HEADS) % BF16_ROWS == 0

    slopes = jnp.exp2(-8.0 * jnp.arange(1, A_HEADS + 1, dtype=F32) / A_HEADS)
    row = jnp.arange(dec_seq * A_HEADS)
    slope_rows = jnp.broadcast_to(slopes[row % A_HEADS][:, None], (row.size, LANES))
    tq_rows = jnp.broadcast_to((past + row // A_HEADS).astype(F32)[:, None], (row.size, LANES))
    ck_t = jnp.transpose(cache_k, (0, 1, 3, 4, 2)).reshape(depth, n_pool, A_WIDTH, PAGE_SIZE)
    cv_t = jnp.transpose(cache_v, (0, 1, 3, 4, 2)).reshape(depth, n_pool, A_WIDTH, PAGE_SIZE)

    hp = x_prompt.reshape(Bp * L, D_MODEL)
    hs = jnp.pad(x_sample, ((0, 0), (0, tpad - dec_seq), (0, 0))).reshape(Bs * tpad, D_MODEL)
    zeros = lambda *s: jnp.zeros(s, F32)
    tm_p = 1024 if (Bp * L) % 1024 == 0 else MOBA_BLOCK
    tm_s = Bs * tpad
    tc_p = 256 if L % 256 == 0 else L
    outs = [[] for _ in range(12)]
    for l in range(depth):
        w_re = _rearrange_w_in(w_in[l])
        wa_bd = _block_diag(lru_wa[l]).astype(BF16)
        wx_bd = _block_diag(lru_wx[l]).astype(BF16)
        alog_e = jnp.repeat(dn_a_log[l], C_VAL_DIM).reshape(1, C_V_WIDTH)
        dtb_e = jnp.repeat(dn_dt_bias[l], C_VAL_DIM).reshape(1, C_V_WIDTH)
        wpa, wpb, wpc, wout = (w.astype(BF16) for w in (w_pa[l], w_pb[l], w_pc[l], w_out[l]))
        last = l == depth - 1
        lru_w = (conv_b_w[l], conv_b_b[l], wa_bd, lru_ba[l], wx_bd, lru_bx[l], lru_lambda[l])
        gdn_w = (conv_c_w[l], alog_e, dtb_e, dn_norm_w[l])

        u = _inproj(hp, norm_w[l], w_re, tm_p)
        u3 = u.reshape(Bp, L, N_U)
        oa = _moba_prompt(u, slopes, Bp, L)
        ob, cbp, hbp = _lru(u3, zeros(Bp, CONV_WIDTH - 1, B_WIDTH), zeros(Bp, B_WIDTH), *lru_w, tc_p, L)
        oc, ccp, scp = _gdn(u3, zeros(Bp, CONV_WIDTH - 1, C_CONV_DIM),
                            zeros(Bp, C_HEADS, C_KEY_DIM, C_VAL_DIM), *gdn_w, tc_p, L)
        hp = _merge(hp, oa, u, ob.reshape(Bp * L, B_WIDTH), oc.reshape(Bp * L, C_V_WIDTH),
                    wpa, wpb, wpc, wout, final_norm_w, tm_p // 2, last)
        kp = u3[:, :, CB_KA * COL:(CB_KA + 1) * COL].reshape(Bp, L, A_HEADS, A_HEAD_DIM)
        vp = u3[:, :, CB_VA * COL:(CB_VA + 1) * COL].reshape(Bp, L, A_HEADS, A_HEAD_DIM)

        us = _inproj(hs, norm_w[l], w_re, tm_s)
        us3 = us.reshape(Bs, tpad, N_U)
        oas = _moba_sample(us, ck_t, cv_t, page_table, slope_rows, tq_rows, l, Bs, tpad, dec_seq)
        obs, cbs, hbs = _lru(us3, state_conv_b[l], state_h_b[l], *lru_w, tpad, dec_seq)
        ocs, ccs, scs = _gdn(us3, state_conv_c[l], state_s_c[l], *gdn_w, tpad, dec_seq)
        hs = _merge(hs, oas, us, obs.reshape(Bs * tpad, B_WIDTH), ocs.reshape(Bs * tpad, C_V_WIDTH),
                    wpa, wpb, wpc, wout, final_norm_w, tm_s, last)
        ks = us3[:, :dec_seq, CB_KA * COL:(CB_KA + 1) * COL].reshape(Bs, dec_seq, A_HEADS, A_HEAD_DIM)
        vs = us3[:, :dec_seq, CB_VA * COL:(CB_VA + 1) * COL].reshape(Bs, dec_seq, A_HEADS, A_HEAD_DIM)

        hbp = hbp.reshape(Bp, B_WIDTH)
        hbs = hbs.reshape(Bs, B_WIDTH)
        for lst, val in zip(outs, (kp, vp, ks, vs, cbp, hbp, cbs, hbs, ccp, scp, ccs, scs)):
            lst.append(val)

    y_prompt = hp.reshape(Bp, L, D_MODEL)
    y_sample = hs.reshape(Bs, tpad, D_MODEL)[:, :dec_seq]
    return (y_prompt, y_sample) + tuple(jnp.stack(o) for o in outs)
```

```python
import functools

import jax
import jax.numpy as jnp
from jax import lax
from jax.experimental import pallas as pl
from jax.experimental.pallas import tpu as pltpu

F32 = jnp.float32
BF16 = jnp.bfloat16
HIGHEST = lax.Precision.HIGHEST

D_MODEL = 1024
A_HEADS = 8
A_HEAD_DIM = 64
A_WIDTH = A_HEADS * A_HEAD_DIM
MOBA_BLOCK = 256
MOBA_TOPK = 3
PAGE_SIZE = 128
B_WIDTH = 512
B_BLOCKS = 8
LRU_C = 8.0
CONV_WIDTH = 4
C_HEADS = 4
C_KEY_DIM = 128
C_VAL_DIM = 128
C_QK_WIDTH = C_HEADS * C_KEY_DIM
C_V_WIDTH = C_HEADS * C_VAL_DIM
C_CONV_DIM = 2 * C_QK_WIDTH + C_V_WIDTH
DELTA_CHUNK = 64
NEUMANN_SPLIT_STEPS = 2
N_BRANCH = 3
EPS = 1e-6

COL = 512
FB_QA, FB_KA, FB_VA, FB_BETA, FB_ALPHA = 0, 1, 2, 3, 4
N_FBLK = 5
HB_QKVC, HB_ZA, HB_GATES, HB_XB, HB_ZB, HB_ZC = 0, 3, 4, 10, 11, 12
N_HBLK = 13
N_UF = N_FBLK * COL
N_UH = N_HBLK * COL

SUBLANES = 8
LANES = 128
BF16_ROWS = 16
TAIL0 = SUBLANES - (CONV_WIDTH - 1)
VMEM_LIMIT = 56 * 1024 * 1024

NT_DIMS = (((1,), (1,)), ((), ()))
TN_DIMS = (((0,), (0,)), ((), ()))


def _cparams(*sem):
    return pltpu.CompilerParams(dimension_semantics=sem, vmem_limit_bytes=VMEM_LIMIT)


def _sigmoid(x):
    return 1.0 / (1.0 + jnp.exp(-x))


def _silu(x):
    return x * _sigmoid(x)


def _softplus(x):
    return jnp.maximum(x, 0.0) + jnp.log1p(jnp.exp(-jnp.abs(x)))


def _mm(a, b):
    return jnp.dot(a, b, preferred_element_type=F32)


def _mm_nt(a, b):
    return lax.dot_general(a, b, NT_DIMS, preferred_element_type=F32)


def _split2(x):
    hi = x.astype(BF16)
    lo = (x - hi.astype(F32)).astype(BF16)
    return hi, lo


def _mm_split(a, b):
    (ah, al), (bh, bl) = a, b
    return _mm(ah, bh) + (_mm(ah, bl) + _mm(al, bh))


def _inproj_kernel(h_ref, nw_ref, w_ref, uf_ref, uh_ref, xn_sc):
    j = pl.program_id(1)

    @pl.when(j == 0)
    def _():
        x = h_ref[...]
        y = x * lax.rsqrt(jnp.mean(x * x, axis=-1, keepdims=True) + EPS) * nw_ref[...]
        xn_sc[...] = y.astype(BF16)

    acc = jnp.dot(xn_sc[...], w_ref[...], preferred_element_type=F32)

    @pl.when(j < N_FBLK)
    def _():
        uf_ref[...] = acc

    @pl.when(j >= N_FBLK)
    def _():
        uh_ref[...] = acc.astype(BF16)


def _inproj(h2d, norm_w, w_re, tm):
    T = h2d.shape[0]
    return pl.pallas_call(
        _inproj_kernel,
        grid=(T // tm, N_FBLK + N_HBLK),
        in_specs=[pl.BlockSpec((tm, D_MODEL), lambda i, j: (i, 0)),
                  pl.BlockSpec((1, D_MODEL), lambda i, j: (0, 0)),
                  pl.BlockSpec((D_MODEL, COL), lambda i, j: (0, j))],
        out_specs=[pl.BlockSpec((tm, COL), lambda i, j: (i, jnp.minimum(j, N_FBLK - 1))),
                   pl.BlockSpec((tm, COL), lambda i, j: (i, jnp.maximum(j - N_FBLK, 0)))],
        out_shape=[jax.ShapeDtypeStruct((T, N_UF), F32), jax.ShapeDtypeStruct((T, N_UH), BF16)],
        scratch_shapes=[pltpu.VMEM((tm, D_MODEL), BF16)],
        compiler_params=_cparams("parallel", "arbitrary"),
        name="inproj",
    )(h2d, norm_w.reshape(1, D_MODEL), w_re)


def _topk_select(gate, n_past, n_iota):
    nb = gate.shape[0]
    g = jnp.where(n_iota < n_past, gate, -jnp.inf)
    rank = jnp.zeros_like(g)
    for m in range(nb):
        gm = g[m:m + 1, :]
        rank = rank + jnp.where(n_iota > m, jnp.where(gm >= g, 1.0, 0.0), jnp.where(gm > g, 1.0, 0.0))
    return jnp.where(n_iota < n_past, jnp.where(rank < MOBA_TOPK, 1.0, 0.0), 0.0)


def _moba_prompt_kernel(slopes_ref, q_ref, k_ref, v_ref, o_ref, kb_sc, vt_sc, means_sc, cmask_sc, sel_sc, s_sc, *,
                        nblk):
    BLK, HD, PW = MOBA_BLOCK, A_HEAD_DIM, 2 * A_HEAD_DIM
    pair = pl.program_id(1)
    i = pl.program_id(2)
    lane = lax.broadcasted_iota(jnp.int32, (1, PW), 1)

    @pl.when(i == 0)
    def _init():
        c_col = lax.broadcasted_iota(jnp.int32, (BLK, PW), 0).astype(F32)
        lane_a = lax.broadcasted_iota(jnp.int32, (BLK, PW), 1)
        key_off = jnp.where(lane_a == 0, c_col, 0.0).astype(BF16)
        ones = jnp.ones((BF16_ROWS, BLK), BF16)
        for n in range(nblk):
            kblk = k_ref[n * BLK:(n + 1) * BLK, :]
            mean = jnp.sum(kblk, axis=0, keepdims=True) * (1.0 / BLK)
            means_sc[n:n + 1, :] = jnp.where(lane < HD, mean, 0.0)
            means_sc[nblk + n:nblk + n + 1, :] = jnp.where(lane < HD, 0.0, mean)
            kb_sc[n, :, 0:PW] = kblk.astype(BF16)
            kb_sc[n, :, PW:2 * PW] = key_off
            vt = v_ref[n * BLK:(n + 1) * BLK, :].T
            for hh in range(2):
                vt_sc[n, hh, 0:HD, :] = vt[hh * HD:(hh + 1) * HD, :].astype(BF16)
                vt_sc[n, hh, HD:HD + BF16_ROWS, :] = ones
        c_io = lax.broadcasted_iota(jnp.int32, (BLK, BLK), 0)
        r_io = lax.broadcasted_iota(jnp.int32, (BLK, BLK), 1)
        cmask_sc[...] = jnp.where(c_io <= r_io, 0.0, -jnp.inf)

    qf = q_ref[...] * (HD ** -0.5)
    gate2 = lax.dot_general(means_sc[...], qf, NT_DIMS, precision=HIGHEST, preferred_element_type=F32)
    n_iota = lax.broadcasted_iota(jnp.int32, (nblk, BLK), 0)
    lane_q = lax.broadcasted_iota(jnp.int32, (BLK, PW), 1)
    qaugs, slopes = [], []
    for hh in range(2):
        slope = slopes_ref[pair * 2 + hh]
        sel = _topk_select(gate2[hh * nblk:(hh + 1) * nblk, :], i, n_iota)
        for n in range(nblk):
            sel_sc[hh, n] = sel[n:n + 1, :]
        qh = jnp.where((lane_q // HD) == hh, qf, 0.0).astype(BF16)
        slope_lane = jnp.where(lane_q == 0, slope, 0.0).astype(BF16)
        qaugs.append(jnp.concatenate([qh, slope_lane], axis=1))
        slopes.append(slope)

    def colmax8(s):
        out = s[0:SUBLANES]
        for r in range(1, BLK // SUBLANES):
            out = jnp.maximum(out, s[r * SUBLANES:(r + 1) * SUBLANES])
        return out

    def past_tile(t, hh):
        j = jnp.minimum(t, i - 1)
        on = sel_sc[hh, j] * jnp.where(t < i, 1.0, 0.0) > 0.0
        delta = -slopes[hh] * ((i - j) * BLK).astype(F32)
        return j, on, delta

    s_own = [_mm_nt(kb_sc[i], qaugs[hh]) + cmask_sc[...] for hh in range(2)]

    def max_trip(tt, mx):
        mx = list(mx)
        for e in range(2):
            for hh in range(2):
                j, on, delta = past_tile(2 * tt + e, hh)
                s = _mm_nt(kb_sc[j], qaugs[hh])
                s_sc[hh, j] = s
                mx[hh] = jnp.maximum(mx[hh], jnp.where(on, colmax8(s) + delta, -jnp.inf))
        return tuple(mx)

    mx = lax.fori_loop(0, (i + 1) // 2, max_trip, tuple(colmax8(s) for s in s_own))
    m = [jnp.max(x, axis=0, keepdims=True) for x in mx]

    def pv_trip(tt, acc):
        acc = list(acc)
        for hh in range(2):
            ps, vts = [], []
            for e in range(2):
                j, on, delta = past_tile(2 * tt + e, hh)
                p = jnp.exp(s_sc[hh, j] - jnp.where(on, m[hh] - delta, jnp.inf))
                ps.append(p.astype(BF16))
                vts.append(vt_sc[j, hh])
            acc[hh] = acc[hh] + _mm(jnp.concatenate(vts, axis=1), jnp.concatenate(ps, axis=0))
        return tuple(acc)

    acc0 = tuple(_mm(vt_sc[i, hh], jnp.exp(s_own[hh] - m[hh]).astype(BF16)) for hh in range(2))
    acc = lax.fori_loop(0, (i + 1) // 2, pv_trip, acc0)
    outs = [a[0:HD, :] / a[HD:HD + 1, :] for a in acc]
    o_ref[...] = jnp.concatenate(outs, axis=0).T


def _moba_prompt(u, slopes, Bn, L):
    nblk = L // MOBA_BLOCK
    PW = 2 * A_HEAD_DIM
    cpb = COL // PW
    return pl.pallas_call(
        functools.partial(_moba_prompt_kernel, nblk=nblk),
        grid=(Bn, A_HEADS // 2, nblk),
        in_specs=[pl.BlockSpec(memory_space=pltpu.SMEM),
                  pl.BlockSpec((MOBA_BLOCK, PW), lambda b, p, i: (b * nblk + i, FB_QA * cpb + p)),
                  pl.BlockSpec((L, PW), lambda b, p, i: (b, FB_KA * cpb + p)),
                  pl.BlockSpec((L, PW), lambda b, p, i: (b, FB_VA * cpb + p))],
        out_specs=pl.BlockSpec((MOBA_BLOCK, PW), lambda b, p, i: (b * nblk + i, p)),
        out_shape=jax.ShapeDtypeStruct((Bn * L, A_WIDTH), F32),
        scratch_shapes=[pltpu.VMEM((nblk, MOBA_BLOCK, 2 * PW), BF16),
                        pltpu.VMEM((nblk, 2, A_HEAD_DIM + BF16_ROWS, MOBA_BLOCK), BF16),
                        pltpu.VMEM((2 * nblk, PW), F32),
                        pltpu.VMEM((MOBA_BLOCK, MOBA_BLOCK), F32),
                        pltpu.VMEM((2, nblk, 1, MOBA_BLOCK), F32),
                        pltpu.VMEM((2, nblk, MOBA_BLOCK, MOBA_BLOCK), F32)],
        compiler_params=_cparams("parallel", "parallel", "arbitrary"),
        name="moba_prompt",
    )(slopes, u, u, u)


PAGES_PER_STEP = 16


def _moba_sample_kernel(pt_ref, slope_ref, tq_ref, q_ref, kn_ref, vn_ref, *refs, n_pages, dec_seq):
    del pt_ref
    PPS = PAGES_PER_STEP
    k_refs = refs[:PPS]
    v_refs = refs[PPS:2 * PPS]
    o_ref = refs[2 * PPS]
    qbd_sc, s_sc, p_sc, acc_sc, linv_sc = refs[2 * PPS + 1:]
    ph = pl.program_id(1)
    st = pl.program_id(2)
    nsteps = n_pages // PPS
    ppb = MOBA_BLOCK // PAGE_SIZE
    nb = n_pages // ppb
    NR = dec_seq * A_HEADS
    head_of_row = lax.broadcasted_iota(jnp.int32, (A_HEADS, A_WIDTH), 0)
    head_of_lane = lax.broadcasted_iota(jnp.int32, (A_HEADS, A_WIDTH), 1) // A_HEAD_DIM
    hmask = jnp.where(head_of_row == head_of_lane, 1.0, 0.0)

    @pl.when((ph == 0) & (st == 0))
    def _init():
        for t in range(dec_seq):
            qrow = q_ref[t:t + 1, :] * (A_HEAD_DIM ** -0.5)
            qbd_sc[t * A_HEADS:(t + 1) * A_HEADS, :] = qrow * hmask

    @pl.when(ph == 0)
    def _scores():
        qbd = qbd_sc[...].astype(BF16)
        for r in range(PPS):
            s_sc[st * PPS + r] = _mm(qbd, k_refs[r][...].astype(BF16))

    @pl.when((ph == 0) & (st == nsteps - 1))
    def _softmax():
        slope = slope_ref[...]
        tq = tq_ref[...]
        lane = lax.broadcasted_iota(jnp.int32, (NR, PAGE_SIZE), 1)
        lane_f = lane.astype(F32)
        q_of_row = lax.broadcasted_iota(jnp.int32, (NR, PAGE_SIZE), 0) // A_HEADS
        gate = jnp.full((NR, PAGE_SIZE), -jnp.inf, F32)
        for n in range(nb):
            tot = s_sc[ppb * n]
            for e in range(1, ppb):
                tot = tot + s_sc[ppb * n + e]
            gate = jnp.where(lane == n, jnp.sum(tot, axis=-1, keepdims=True) * (1.0 / MOBA_BLOCK), gate)
        sel = jnp.zeros((NR, PAGE_SIZE), F32)
        for _ in range(MOBA_TOPK):
            mx = jnp.max(gate, axis=-1, keepdims=True)
            first = jnp.min(jnp.where(gate == mx, lane_f, float(PAGE_SIZE)), axis=-1, keepdims=True)
            pick = lane_f == first
            sel = jnp.where(pick, 1.0, sel)
            gate = jnp.where(pick, -jnp.inf, gate)
        pad = jnp.zeros((PAGE_SIZE - kn_ref.shape[0], A_WIDTH), F32)
        qbd = qbd_sc[...].astype(BF16)
        s_own = _mm_nt(qbd, jnp.concatenate([kn_ref[...], pad], axis=0).astype(BF16))
        own_ok = (lane <= q_of_row) & (lane < dec_seq)
        s_own = jnp.where(own_ok, s_own - slope * (q_of_row - lane).astype(F32), -jnp.inf)
        m = jnp.max(s_own, axis=-1, keepdims=True)
        for n in range(nb):
            on = jnp.sum(jnp.where(lane == n, sel, 0.0), axis=-1, keepdims=True) > 0.0
            for e in range(ppb):
                pg = ppb * n + e
                sb = s_sc[pg] - slope * (tq - (lane_f + float(pg * PAGE_SIZE)))
                sb = jnp.where(on, sb, -jnp.inf)
                s_sc[pg] = sb
                m = jnp.maximum(m, jnp.max(sb, axis=-1, keepdims=True))
        p_own = jnp.exp(s_own - m)
        l = jnp.sum(p_own, axis=-1, keepdims=True)
        for pg in range(n_pages):
            pb = jnp.exp(s_sc[pg] - m)
            l = l + jnp.sum(pb, axis=-1, keepdims=True)
            p_sc[pg] = pb.astype(BF16)
        linv_sc[...] = jnp.broadcast_to(1.0 / l, (NR, LANES))
        v_own = jnp.concatenate([vn_ref[...], pad], axis=0).astype(BF16)
        acc_sc[...] = _mm(p_own.astype(BF16), v_own)

    @pl.when(ph == 1)
    def _values():
        acc = acc_sc[...]
        for r in range(PPS):
            acc = acc + _mm_nt(p_sc[st * PPS + r], v_refs[r][...].astype(BF16))
        acc_sc[...] = acc

    @pl.when((ph == 1) & (st == nsteps - 1))
    def _finish():
        linv = jnp.concatenate([linv_sc[...]] * (A_WIDTH // LANES), axis=1)
        a = acc_sc[...] * linv
        o_ref[...] = jnp.zeros_like(o_ref)
        for t in range(dec_seq):
            o_ref[t:t + 1, :] = jnp.sum(a[t * A_HEADS:(t + 1) * A_HEADS, :] * hmask, axis=0, keepdims=True)


def _moba_sample(u_s, ck_t, cv_t, page_table, slope_rows, tq_rows, layer, Bs, tpad, dec_seq):
    n_pages = page_table.shape[1]
    PPS = PAGES_PER_STEP
    nsteps = n_pages // PPS
    NR = dec_seq * A_HEADS
    u3 = u_s.reshape(Bs, tpad, N_UF)

    def kmap(r):
        return lambda b, ph, st, pt: (layer, pt[b, (st * (1 - ph) + (nsteps - 1) * ph) * PPS + r], 0, 0)

    def vmap_(r):
        return lambda b, ph, st, pt: (layer, pt[b, (st * ph) * PPS + r], 0, 0)

    page_blk = (None, None, A_WIDTH, PAGE_SIZE)
    in_specs = [pl.BlockSpec((NR, LANES), lambda b, ph, st, pt: (0, 0)),
                pl.BlockSpec((NR, LANES), lambda b, ph, st, pt: (0, 0)),
                pl.BlockSpec((None, tpad, COL), lambda b, ph, st, pt: (b, 0, FB_QA)),
                pl.BlockSpec((None, tpad, COL), lambda b, ph, st, pt: (b, 0, FB_KA)),
                pl.BlockSpec((None, tpad, COL), lambda b, ph, st, pt: (b, 0, FB_VA))]
    in_specs += [pl.BlockSpec(page_blk, kmap(r)) for r in range(PPS)]
    in_specs += [pl.BlockSpec(page_blk, vmap_(r)) for r in range(PPS)]
    grid_spec = pltpu.PrefetchScalarGridSpec(
        num_scalar_prefetch=1,
        grid=(Bs, 2, nsteps),
        in_specs=in_specs,
        out_specs=pl.BlockSpec((None, tpad, A_WIDTH), lambda b, ph, st, pt: (b, 0, 0)),
        scratch_shapes=[pltpu.VMEM((NR, A_WIDTH), F32),
                        pltpu.VMEM((n_pages, NR, PAGE_SIZE), F32),
                        pltpu.VMEM((n_pages, NR, PAGE_SIZE), BF16),
                        pltpu.VMEM((NR, A_WIDTH), F32),
                        pltpu.VMEM((NR, LANES), F32)])
    out = pl.pallas_call(
        functools.partial(_moba_sample_kernel, n_pages=n_pages, dec_seq=dec_seq),
        grid_spec=grid_spec,
        out_shape=jax.ShapeDtypeStruct((Bs, tpad, A_WIDTH), F32),
        compiler_params=_cparams("parallel", "arbitrary", "arbitrary"),
        name="moba_sample",
    )(page_table, slope_rows, tq_rows, u3, u3, u3, *([ck_t] * PPS), *([cv_t] * PPS))
    return out.reshape(Bs * tpad, A_WIDTH)


def _lru_kernel(xb_ref, zb_ref, conv0_ref, h0_ref, cw_ref, cb_ref, wa_ref, ba_ref, wx_ref, bx_ref, lam_ref,
                y_ref, conv1_ref, h1_ref, xp_sc, h_sc, a_sc, b_sc, *, Bn, tc, l_valid):
    c = pl.program_id(0)

    @pl.when(c == 0)
    def _():
        xp_sc[:, TAIL0:SUBLANES, :] = conv0_ref[...]
        h_sc[...] = h0_ref[...]

    xp_sc[:, SUBLANES:SUBLANES + tc, :] = xb_ref[...].astype(F32)
    xc = xp_sc[:, TAIL0:TAIL0 + tc, :] * cw_ref[0:1, :]
    for j in range(1, CONV_WIDTH):
        xc = xc + xp_sc[:, TAIL0 + j:TAIL0 + j + tc, :] * cw_ref[j:j + 1, :]
    xc = xc + cb_ref[...]
    nv = min(tc, l_valid)
    tail = xp_sc[:, TAIL0 + nv:SUBLANES + nv, :]
    conv1_ref[...] = tail
    xp_sc[:, TAIL0:SUBLANES, :] = tail

    x2 = xc.reshape(Bn * tc, B_WIDTH)
    x16 = x2.astype(BF16)
    r = _sigmoid(jnp.dot(x16, wa_ref[...], preferred_element_type=F32) + ba_ref[...])
    ig = _sigmoid(jnp.dot(x16, wx_ref[...], preferred_element_type=F32) + bx_ref[...])
    log_a = -LRU_C * r * _softplus(-lam_ref[...])
    a = jnp.exp(log_a)
    bin_ = jnp.sqrt(-jnp.tanh(log_a) * (a * a + 1.0)) * (ig * x2)
    a_sc[...] = a.reshape(Bn, tc, B_WIDTH)
    b_sc[...] = bin_.reshape(Bn, tc, B_WIDTH)

    def step(t, h):
        h = a_sc[:, pl.ds(t, 1), :] * h + b_sc[:, pl.ds(t, 1), :]
        a_sc[:, pl.ds(t, 1), :] = h
        return h

    h = lax.fori_loop(0, nv, step, h_sc[...])
    h_sc[...] = h
    h1_ref[...] = h
    y_ref[...] = _silu(zb_ref[...].astype(F32)) * a_sc[...]


def _lru(u3, conv0, h0, cw, cb, wa_bd, ba, wx_bd, bx, lam, tc, l_valid):
    Bn, Lp, _ = u3.shape
    row = lambda v: v.reshape(1, B_WIDTH)
    const2 = lambda c: (0, 0)
    return pl.pallas_call(
        functools.partial(_lru_kernel, Bn=Bn, tc=tc, l_valid=l_valid),
        grid=(Lp // tc,),
        in_specs=[pl.BlockSpec((Bn, tc, COL), lambda c: (0, c, HB_XB)),
                  pl.BlockSpec((Bn, tc, COL), lambda c: (0, c, HB_ZB)),
                  pl.BlockSpec((Bn, CONV_WIDTH - 1, B_WIDTH), lambda c: (0, 0, 0)),
                  pl.BlockSpec((Bn, 1, B_WIDTH), lambda c: (0, 0, 0)),
                  pl.BlockSpec((CONV_WIDTH, B_WIDTH), const2),
                  pl.BlockSpec((1, B_WIDTH), const2),
                  pl.BlockSpec((B_WIDTH, B_WIDTH), const2),
                  pl.BlockSpec((1, B_WIDTH), const2),
                  pl.BlockSpec((B_WIDTH, B_WIDTH), const2),
                  pl.BlockSpec((1, B_WIDTH), const2),
                  pl.BlockSpec((1, B_WIDTH), const2)],
        out_specs=[pl.BlockSpec((Bn, tc, B_WIDTH), lambda c: (0, c, 0)),
                   pl.BlockSpec((Bn, CONV_WIDTH - 1, B_WIDTH), lambda c: (0, 0, 0)),
                   pl.BlockSpec((Bn, 1, B_WIDTH), lambda c: (0, 0, 0))],
        out_shape=[jax.ShapeDtypeStruct((Bn, Lp, B_WIDTH), F32),
                   jax.ShapeDtypeStruct((Bn, CONV_WIDTH - 1, B_WIDTH), F32),
                   jax.ShapeDtypeStruct((Bn, 1, B_WIDTH), F32)],
        scratch_shapes=[pltpu.VMEM((Bn, SUBLANES + tc, B_WIDTH), F32),
                        pltpu.VMEM((Bn, 1, B_WIDTH), F32),
                        pltpu.VMEM((Bn, tc, B_WIDTH), F32),
                        pltpu.VMEM((Bn, tc, B_WIDTH), F32)],
        compiler_params=_cparams("arbitrary"),
        name="rglru",
    )(u3, u3, conv0, h0.reshape(Bn, 1, B_WIDTH), cw, row(cb), wa_bd, row(ba), wx_bd, row(bx), row(lam))


def _gdn_kernel(qkv_ref, beta_ref, alpha_ref, z_ref, conv0_ref, s0_ref, cw_ref, alog_ref, dtb_ref, nw_ref,
                o_ref, conv1_ref, s1_ref, xp_sc, s_sc, y_sc, *, tb, l_valid):
    C = DELTA_CHUNK
    R = max(tb, C)
    nc = R // C
    t = pl.program_id(1)

    @pl.when(t == 0)
    def _():
        xp_sc[TAIL0:SUBLANES, :] = conv0_ref[...]
        s_sc[...] = s0_ref[...]

    xp_sc[SUBLANES:SUBLANES + tb, :] = qkv_ref[...].astype(F32)
    y = xp_sc[TAIL0:TAIL0 + tb, :] * cw_ref[0:1, :]
    for j in range(1, CONV_WIDTH):
        y = y + xp_sc[TAIL0 + j:TAIL0 + j + tb, :] * cw_ref[j:j + 1, :]
    nv = min(tb, l_valid)
    tail = xp_sc[TAIL0 + nv:SUBLANES + nv, :]
    conv1_ref[...] = tail
    xp_sc[TAIL0:SUBLANES, :] = tail
    y = _silu(y)

    if nv < R:
        live = lax.broadcasted_iota(jnp.int32, (tb, 1), 0) < nv
        y_sc[...] = jnp.zeros_like(y_sc)
        y_sc[0:tb, :] = jnp.where(live, y, 0.0)
    else:
        live = None
        y_sc[...] = y

    ri = lax.broadcasted_iota(jnp.int32, (R, R), 0)
    ci = lax.broadcasted_iota(jnp.int32, (R, R), 1)
    same = (ri // C) == (ci // C)
    causal = same & (ri >= ci)
    strict = same & (ri > ci)
    diag = ri == ci
    eye = jnp.where(diag, 1.0, 0.0)
    ltri16 = jnp.where(causal, 1.0, 0.0).astype(BF16)

    def padrows(v):
        if tb == R:
            return v
        return jnp.concatenate([v, jnp.zeros((R - tb, v.shape[1]), F32)], axis=0)

    def widen(v):
        return v[:, 0:R] if R <= LANES else jnp.concatenate([v] * (R // LANES), axis=1)

    heads = range(C_HEADS)
    hsl = [slice(h * C_KEY_DIM, (h + 1) * C_KEY_DIM) for h in heads]
    st = []
    for h in heads:
        q = y_sc[:, h * C_KEY_DIM:(h + 1) * C_KEY_DIM]
        k = y_sc[:, C_QK_WIDTH + h * C_KEY_DIM:C_QK_WIDTH + (h + 1) * C_KEY_DIM]
        v = y_sc[:, 2 * C_QK_WIDTH + h * C_VAL_DIM:2 * C_QK_WIDTH + (h + 1) * C_VAL_DIM]
        q = q * lax.rsqrt(jnp.sum(q * q, axis=-1, keepdims=True) + EPS) * (C_KEY_DIM ** -0.5)
        k = k * lax.rsqrt(jnp.sum(k * k, axis=-1, keepdims=True) + EPS)
        beta = _sigmoid(beta_ref[:, hsl[h]])
        g = -jnp.exp(alog_ref[:, hsl[h]]) * _softplus(alpha_ref[:, hsl[h]] + dtb_ref[:, hsl[h]])
        if live is not None:
            beta = jnp.where(live, beta, 0.0)
            g = jnp.where(live, g, 0.0)
        beta = padrows(beta)
        g = padrows(g)
        g_hi, g_lo = _split2(g)
        g_lo2 = (g - g_hi.astype(F32) - g_lo.astype(F32)).astype(BF16)
        G = _mm(ltri16, g_hi) + (_mm(ltri16, g_lo) + _mm(ltri16, g_lo2))
        Gcol = widen(G)
        Grow = jnp.sum(jnp.where(diag, Gcol, 0.0), axis=0, keepdims=True)
        decay = jnp.exp(jnp.where(causal, Gcol - Grow, -jnp.inf))
        kb = k * beta
        k16 = k.astype(BF16)
        A = jnp.where(strict, _mm_nt(kb.astype(BF16), k16) * decay, 0.0)
        eG = jnp.exp(G)
        rhs16 = jnp.concatenate([v * beta, kb * eG], axis=1).astype(BF16)
        QK16 = (_mm_nt(q.astype(BF16), k16) * decay).astype(BF16)
        glast = jnp.concatenate([jnp.broadcast_to(G[(c + 1) * C - 1:(c + 1) * C, :], (C, LANES)) for c in range(nc)],
                                axis=0)
        st.append(dict(M=-A, rhs16=rhs16, QK16=QK16, qg16=(q * eG).astype(BF16),
                       kg16=(k * jnp.exp(glast - G)).astype(BF16),
                       gl=[jnp.exp(G[(c + 1) * C - 1:(c + 1) * C, :]) for c in range(nc)]))

    for s in st:
        s["T"] = eye + s["M"]
        s["Ms"] = _split2(s["M"])
    for it in range(5):
        if it < NEUMANN_SPLIT_STEPS:
            for s in st:
                s["M"] = _mm_split(s["Ms"], s["Ms"])
            for s in st:
                s["Ms"] = _split2(s["M"])
                s["T"] = s["T"] + _mm_split(_split2(s["T"]), s["Ms"])
        else:
            for s in st:
                s["M"] = _mm(s["Ms"][0], s["Ms"][0])
            for s in st:
                s["Ms"] = (s["M"].astype(BF16),)
                s["T"] = s["T"] + _mm(s["T"].astype(BF16), s["Ms"][0])
    for h, s in zip(heads, st):
        UW = _mm(s["T"].astype(BF16), s["rhs16"])
        s["U"] = UW[:, 0:C_VAL_DIM]
        s["W16"] = UW[:, C_VAL_DIM:].astype(BF16)
        s["S"] = s_sc[h]
        s["vn"], s["oi"] = [], []

    for c in range(nc):
        rs = slice(c * C, (c + 1) * C)
        for s in st:
            S16 = s["S"].astype(BF16)
            vn = s["U"][rs] - _mm(s["W16"][rs], S16)
            s["oi"].append(_mm(s["qg16"][rs], S16))
            s["S"] = s["S"] * s["gl"][c] + lax.dot_general(s["kg16"][rs], vn.astype(BF16), TN_DIMS,
                                                           preferred_element_type=F32)
            s["vn"].append(vn)
    for h, s in zip(heads, st):
        s_sc[h] = s["S"]
        s1_ref[h] = s["S"]
        o = jnp.concatenate(s["oi"], axis=0) + _mm(s["QK16"], jnp.concatenate(s["vn"], axis=0).astype(BF16))
        o = o * lax.rsqrt(jnp.mean(o * o, axis=-1, keepdims=True) + EPS) * nw_ref[...]
        o_ref[:, hsl[h]] = o[0:tb] * _silu(z_ref[:, hsl[h]].astype(F32))


def _gdn(uf3, uh3, conv0, s0, cw, alog_e, dtb_e, nw, tb, l_valid):
    Bn, Lp, _ = uf3.shape
    rows = max(tb, DELTA_CHUNK)
    const2 = lambda b, t: (0, 0)
    return pl.pallas_call(
        functools.partial(_gdn_kernel, tb=tb, l_valid=l_valid),
        grid=(Bn, Lp // tb),
        in_specs=[pl.BlockSpec((None, tb, C_CONV_DIM), lambda b, t: (b, t, HB_QKVC * COL // C_CONV_DIM)),
                  pl.BlockSpec((None, tb, COL), lambda b, t: (b, t, FB_BETA)),
                  pl.BlockSpec((None, tb, COL), lambda b, t: (b, t, FB_ALPHA)),
                  pl.BlockSpec((None, tb, COL), lambda b, t: (b, t, HB_ZC)),
                  pl.BlockSpec((None, CONV_WIDTH - 1, C_CONV_DIM), lambda b, t: (b, 0, 0)),
                  pl.BlockSpec((None, C_HEADS, C_KEY_DIM, C_VAL_DIM), lambda b, t: (b, 0, 0, 0)),
                  pl.BlockSpec((CONV_WIDTH, C_CONV_DIM), const2),
                  pl.BlockSpec((1, C_V_WIDTH), const2),
                  pl.BlockSpec((1, C_V_WIDTH), const2),
                  pl.BlockSpec((1, C_VAL_DIM), const2)],
        out_specs=[pl.BlockSpec((None, tb, C_V_WIDTH), lambda b, t: (b, t, 0)),
                   pl.BlockSpec((None, CONV_WIDTH - 1, C_CONV_DIM), lambda b, t: (b, 0, 0)),
                   pl.BlockSpec((None, C_HEADS, C_KEY_DIM, C_VAL_DIM), lambda b, t: (b, 0, 0, 0))],
        out_shape=[jax.ShapeDtypeStruct((Bn, Lp, C_V_WIDTH), F32),
                   jax.ShapeDtypeStruct((Bn, CONV_WIDTH - 1, C_CONV_DIM), F32),
                   jax.ShapeDtypeStruct((Bn, C_HEADS, C_KEY_DIM, C_VAL_DIM), F32)],
        scratch_shapes=[pltpu.VMEM((SUBLANES + tb, C_CONV_DIM), F32),
                        pltpu.VMEM((C_HEADS, C_KEY_DIM, C_VAL_DIM), F32),
                        pltpu.VMEM((rows, C_CONV_DIM), F32)],
        compiler_params=_cparams("parallel", "arbitrary"),
        name="gdn",
    )(uh3, uf3, uf3, uh3, conv0, s0, cw, alog_e, dtb_e, nw.reshape(1, C_VAL_DIM))


def _merge_kernel(h_ref, oa_ref, za_ref, ob_ref, oc_ref, g0_ref, g1_ref, g2_ref,
                  wpa_ref, wpb_ref, wpc_ref, wout_ref, fnw_ref, hn_ref, *, final_norm):
    def proj(x, w_ref):
        return jnp.dot(x.astype(BF16), w_ref[...], preferred_element_type=F32)

    f32 = lambda ref: ref[...].astype(F32)
    ya = proj(_silu(f32(za_ref)) * oa_ref[...], wpa_ref)
    yb = proj(ob_ref[...], wpb_ref)
    yc = proj(oc_ref[...], wpc_ref)
    merged = _sigmoid(f32(g0_ref)) * ya + _sigmoid(f32(g1_ref)) * yb + _sigmoid(f32(g2_ref)) * yc
    hn = h_ref[...] + proj(merged, wout_ref)
    if final_norm:
        hn = hn * lax.rsqrt(jnp.mean(hn * hn, axis=-1, keepdims=True) + EPS) * fnw_ref[...]
    hn_ref[...] = hn


def _merge(h2d, oa, uh, ob, oc, wpa, wpb, wpc, wout, fnw, tm, final_norm):
    T = h2d.shape[0]
    gpb = D_MODEL // COL
    tok = lambda w: pl.BlockSpec((tm, w), lambda i: (i, 0))
    wspec = lambda k: pl.BlockSpec((k, D_MODEL), lambda i: (0, 0))
    gate_spec = lambda g: pl.BlockSpec((tm, D_MODEL), lambda i: (i, HB_GATES // gpb + g))
    return pl.pallas_call(
        functools.partial(_merge_kernel, final_norm=final_norm),
        grid=(T // tm,),
        in_specs=[tok(D_MODEL), tok(A_WIDTH),
                  pl.BlockSpec((tm, COL), lambda i: (i, HB_ZA)),
                  tok(B_WIDTH), tok(C_V_WIDTH),
                  gate_spec(0), gate_spec(1), gate_spec(2),
                  wspec(A_WIDTH), wspec(B_WIDTH), wspec(C_V_WIDTH), wspec(D_MODEL),
                  pl.BlockSpec((1, D_MODEL), lambda i: (0, 0))],
        out_specs=tok(D_MODEL),
        out_shape=jax.ShapeDtypeStruct((T, D_MODEL), F32),
        compiler_params=_cparams("parallel"),
        name="merge",
    )(h2d, oa, uh, ob, oc, uh, uh, uh, wpa, wpb, wpc, wout, fnw.reshape(1, D_MODEL))


def _block_diag(w):
    g, n, _ = w.shape
    out = jnp.zeros((g * n, g * n), w.dtype)
    for i in range(g):
        out = out.at[i * n:(i + 1) * n, i * n:(i + 1) * n].set(w[i])
    return out


def _rearrange_w_in(w):
    sizes = (A_WIDTH, A_WIDTH, A_WIDTH, A_WIDTH, B_WIDTH, B_WIDTH, C_CONV_DIM, C_V_WIDTH, C_HEADS, C_HEADS,
             N_BRANCH * D_MODEL)
    offs = [0]
    for s in sizes:
        offs.append(offs[-1] + s)
    qa, ka, va, za, xb, zb, qkvc, zc, beta, alpha, gates = (w[:, offs[n]:offs[n + 1]] for n in range(len(sizes)))
    rep = lambda c: jnp.repeat(c, C_VAL_DIM, axis=1)
    return jnp.concatenate([qa, ka, va, rep(beta), rep(alpha), qkvc, za, gates, xb, zb, zc], axis=1).astype(BF16)


def kernel(x_prompt, x_sample, cache_k, cache_v, page_table, state_conv_b, state_h_b, state_conv_c, state_s_c,
           norm_w, w_in, conv_b_w, conv_b_b, lru_wa, lru_ba, lru_wx, lru_bx, lru_lambda,
           conv_c_w, dn_a_log, dn_dt_bias, dn_norm_w, w_pa, w_pb, w_pc, w_out, final_norm_w):
    Bp, L, _ = x_prompt.shape
    Bs, dec_seq, _ = x_sample.shape
    depth = w_in.shape[0]
    n_pool = cache_k.shape[1]
    n_pages = page_table.shape[1]
    past = n_pages * PAGE_SIZE
    tpad = BF16_ROWS
    assert dec_seq <= tpad and L % MOBA_BLOCK == 0 and past % MOBA_BLOCK == 0 and n_pages % PAGES_PER_STEP == 0
    assert MOBA_TOPK <= past // MOBA_BLOCK <= PAGE_SIZE and (dec_seq * A_HEADS) % BF16_ROWS == 0

    slopes = jnp.exp2(-8.0 * jnp.arange(1, A_HEADS + 1, dtype=F32) / A_HEADS)
    row = jnp.arange(dec_seq * A_HEADS)
    slope_rows = jnp.broadcast_to(slopes[row % A_HEADS][:, None], (row.size, LANES))
    tq_rows = jnp.broadcast_to((past + row // A_HEADS).astype(F32)[:, None], (row.size, LANES))
    ck_t = jnp.transpose(cache_k, (0, 1, 3, 4, 2)).reshape(depth, n_pool, A_WIDTH, PAGE_SIZE)
    cv_t = jnp.transpose(cache_v, (0, 1, 3, 4, 2)).reshape(depth, n_pool, A_WIDTH, PAGE_SIZE)

    hp = x_prompt.reshape(Bp * L, D_MODEL)
    hs = jnp.pad(x_sample, ((0, 0), (0, tpad - dec_seq), (0, 0))).reshape(Bs * tpad, D_MODEL)
    zeros = lambda *s: jnp.zeros(s, F32)
    tm_p = 2048 if (Bp * L) % 2048 == 0 else MOBA_BLOCK
    tm_merge = min(tm_p, 512)
    tm_s = Bs * tpad
    tc_p = 256 if L % 256 == 0 else L
    outs = [[] for _ in range(12)]
    for l in range(depth):
        w_re = _rearrange_w_in(w_in[l])
        wa_bd = _block_diag(lru_wa[l]).astype(BF16)
        wx_bd = _block_diag(lru_wx[l]).astype(BF16)
        alog_e = jnp.repeat(dn_a_log[l], C_VAL_DIM).reshape(1, C_V_WIDTH)
        dtb_e = jnp.repeat(dn_dt_bias[l], C_VAL_DIM).reshape(1, C_V_WIDTH)
        wpa, wpb, wpc, wout = (w.astype(BF16) for w in (w_pa[l], w_pb[l], w_pc[l], w_out[l]))
        last = l == depth - 1
        lru_w = (conv_b_w[l], conv_b_b[l], wa_bd, lru_ba[l], wx_bd, lru_bx[l], lru_lambda[l])
        gdn_w = (conv_c_w[l], alog_e, dtb_e, dn_norm_w[l])

        uf, uh = _inproj(hp, norm_w[l], w_re, tm_p)
        uf3 = uf.reshape(Bp, L, N_UF)
        uh3 = uh.reshape(Bp, L, N_UH)
        oa = _moba_prompt(uf, slopes, Bp, L)
        ob, cbp, hbp = _lru(uh3, zeros(Bp, CONV_WIDTH - 1, B_WIDTH), zeros(Bp, B_WIDTH), *lru_w, tc_p, L)
        oc, ccp, scp = _gdn(uf3, uh3, zeros(Bp, CONV_WIDTH - 1, C_CONV_DIM),
                            zeros(Bp, C_HEADS, C_KEY_DIM, C_VAL_DIM), *gdn_w, tc_p, L)
        hp = _merge(hp, oa, uh, ob.reshape(Bp * L, B_WIDTH), oc.reshape(Bp * L, C_V_WIDTH),
                    wpa, wpb, wpc, wout, final_norm_w, tm_merge, last)
        kp = uf3[:, :, FB_KA * COL:(FB_KA + 1) * COL].reshape(Bp, L, A_HEADS, A_HEAD_DIM)
        vp = uf3[:, :, FB_VA * COL:(FB_VA + 1) * COL].reshape(Bp, L, A_HEADS, A_HEAD_DIM)

        usf, ush = _inproj(hs, norm_w[l], w_re, tm_s)
        usf3 = usf.reshape(Bs, tpad, N_UF)
        ush3 = ush.reshape(Bs, tpad, N_UH)
        oas = _moba_sample(usf, ck_t, cv_t, page_table, slope_rows, tq_rows, l, Bs, tpad, dec_seq)
        obs, cbs, hbs = _lru(ush3, state_conv_b[l], state_h_b[l], *lru_w, tpad, dec_seq)
        ocs, ccs, scs = _gdn(usf3, ush3, state_conv_c[l], state_s_c[l], *gdn_w, tpad, dec_seq)
        hs = _merge(hs, oas, ush, obs.reshape(Bs * tpad, B_WIDTH), ocs.reshape(Bs * tpad, C_V_WIDTH),
                    wpa, wpb, wpc, wout, final_norm_w, tm_s, last)
        ks = usf3[:, :dec_seq, FB_KA * COL:(FB_KA + 1) * COL].reshape(Bs, dec_seq, A_HEADS, A_HEAD_DIM)
        vs = usf3[:, :dec_seq, FB_VA * COL:(FB_VA + 1) * COL].reshape(Bs, dec_seq, A_HEADS, A_HEAD_DIM)

        hbp = hbp.reshape(Bp, B_WIDTH)
        hbs = hbs.reshape(Bs, B_WIDTH)
        for lst, val in zip(outs, (kp, vp, ks, vs, cbp, hbp, cbs, hbs, ccp, scp, ccs, scs)):
            lst.append(val)

    y_prompt = hp.reshape(Bp, L, D_MODEL)
    y_sample = hs.reshape(Bs, tpad, D_MODEL)[:, :dec_seq]
    return (y_prompt, y_sample) + tuple(jnp.stack(o) for o in outs)
```

```python
import functools

import jax
import jax.numpy as jnp
from jax import lax
from jax.experimental import pallas as pl
from jax.experimental.pallas import tpu as pltpu

F32 = jnp.float32
BF16 = jnp.bfloat16
HIGHEST = lax.Precision.HIGHEST

D_MODEL = 1024
A_HEADS = 8
A_HEAD_DIM = 64
A_WIDTH = A_HEADS * A_HEAD_DIM
MOBA_BLOCK = 256
MOBA_TOPK = 3
PAGE_SIZE = 128
B_WIDTH = 512
B_BLOCKS = 8
LRU_C = 8.0
CONV_WIDTH = 4
C_HEADS = 4
C_KEY_DIM = 128
C_VAL_DIM = 128
C_QK_WIDTH = C_HEADS * C_KEY_DIM
C_V_WIDTH = C_HEADS * C_VAL_DIM
C_CONV_DIM = 2 * C_QK_WIDTH + C_V_WIDTH
DELTA_CHUNK = 64
NEUMANN_SPLIT_STEPS = 2
N_BRANCH = 3
EPS = 1e-6

COL = 512
FB_QA, FB_KA, FB_VA, FB_BETA, FB_ALPHA = 0, 1, 2, 3, 4
N_FBLK = 5
HB_QKVC, HB_ZA, HB_GATES, HB_XB, HB_ZB, HB_ZC = 0, 3, 4, 10, 11, 12
N_HBLK = 13
N_UF = N_FBLK * COL
N_UH = N_HBLK * COL

SUBLANES = 8
LANES = 128
BF16_ROWS = 16
TAIL0 = SUBLANES - (CONV_WIDTH - 1)
VMEM_LIMIT = 56 * 1024 * 1024

NT_DIMS = (((1,), (1,)), ((), ()))
TN_DIMS = (((0,), (0,)), ((), ()))


def _cparams(*sem):
    return pltpu.CompilerParams(dimension_semantics=sem, vmem_limit_bytes=VMEM_LIMIT)


def _sigmoid(x):
    return 1.0 / (1.0 + jnp.exp(-x))


def _silu(x):
    return x * _sigmoid(x)


def _softplus(x):
    return jnp.maximum(x, 0.0) + jnp.log1p(jnp.exp(-jnp.abs(x)))


def _mm(a, b):
    return jnp.dot(a, b, preferred_element_type=F32)


def _mm_nt(a, b):
    return lax.dot_general(a, b, NT_DIMS, preferred_element_type=F32)


def _split2(x):
    hi = x.astype(BF16)
    lo = (x - hi.astype(F32)).astype(BF16)
    return hi, lo


def _mm_split(a, b):
    (ah, al), (bh, bl) = a, b
    return _mm(ah, bh) + (_mm(ah, bl) + _mm(al, bh))


def _inproj_kernel(h_ref, nw_ref, w_ref, uf_ref, uh_ref, xn_sc):
    j = pl.program_id(1)

    @pl.when(j == 0)
    def _():
        x = h_ref[...]
        y = x * lax.rsqrt(jnp.mean(x * x, axis=-1, keepdims=True) + EPS) * nw_ref[...]
        xn_sc[...] = y.astype(BF16)

    acc = jnp.dot(xn_sc[...], w_ref[...], preferred_element_type=F32)

    @pl.when(j < N_FBLK)
    def _():
        uf_ref[...] = acc

    @pl.when(j >= N_FBLK)
    def _():
        uh_ref[...] = acc.astype(BF16)


def _inproj(h2d, norm_w, w_re, tm):
    T = h2d.shape[0]
    return pl.pallas_call(
        _inproj_kernel,
        grid=(T // tm, N_FBLK + N_HBLK),
        in_specs=[pl.BlockSpec((tm, D_MODEL), lambda i, j: (i, 0)),
                  pl.BlockSpec((1, D_MODEL), lambda i, j: (0, 0)),
                  pl.BlockSpec((D_MODEL, COL), lambda i, j: (0, j))],
        out_specs=[pl.BlockSpec((tm, COL), lambda i, j: (i, jnp.minimum(j, N_FBLK - 1))),
                   pl.BlockSpec((tm, COL), lambda i, j: (i, jnp.maximum(j - N_FBLK, 0)))],
        out_shape=[jax.ShapeDtypeStruct((T, N_UF), F32), jax.ShapeDtypeStruct((T, N_UH), BF16)],
        scratch_shapes=[pltpu.VMEM((tm, D_MODEL), BF16)],
        compiler_params=_cparams("parallel", "arbitrary"),
        name="inproj",
    )(h2d, norm_w.reshape(1, D_MODEL), w_re)


TILES_WIDE = 4
TILES_NARROW = 2

def _topk_select(gate, n_past, n_iota):
    nb = gate.shape[0]
    g = jnp.where(n_iota < n_past, gate, -jnp.inf)
    rank = jnp.zeros_like(g)
    for m in range(min(nb, n_past) if isinstance(n_past, int) else nb):
        gm = g[m:m + 1, :]
        rank = rank + jnp.where(n_iota > m, jnp.where(gm >= g, 1.0, 0.0), jnp.where(gm > g, 1.0, 0.0))
    return jnp.where(n_iota < n_past, jnp.where(rank < MOBA_TOPK, 1.0, 0.0), 0.0)


def _moba_prompt_body(pair, i, slopes_ref, q_ref, k_ref, v_ref, o_ref, kb_sc, vt_sc, means_sc, cmask_sc, sel_sc, s_sc,
                      qt_sc, *, nblk):
    BLK, HD, PW = MOBA_BLOCK, A_HEAD_DIM, 2 * A_HEAD_DIM
    lane = lax.broadcasted_iota(jnp.int32, (1, PW), 1)

    @pl.when(i == 0)
    def _init():
        c_col = lax.broadcasted_iota(jnp.int32, (BLK, PW), 0).astype(F32)
        lane_a = lax.broadcasted_iota(jnp.int32, (BLK, PW), 1)
        key_off = jnp.where(lane_a == 0, c_col, 0.0).astype(BF16)
        ones = jnp.ones((BF16_ROWS, BLK), BF16)
        for n in range(nblk):
            kblk = k_ref[n * BLK:(n + 1) * BLK, :]
            mean = jnp.sum(kblk, axis=0, keepdims=True) * (1.0 / BLK)
            means_sc[n:n + 1, :] = jnp.where(lane < HD, mean, 0.0)
            means_sc[nblk + n:nblk + n + 1, :] = jnp.where(lane < HD, 0.0, mean)
            kb_sc[n, :, 0:PW] = kblk.astype(BF16)
            kb_sc[n, :, PW:2 * PW] = key_off
            vt = v_ref[n * BLK:(n + 1) * BLK, :].T
            for hh in range(2):
                vt_sc[n, hh, 0:HD, :] = vt[hh * HD:(hh + 1) * HD, :].astype(BF16)
                vt_sc[n, hh, HD:HD + BF16_ROWS, :] = ones
        c_io = lax.broadcasted_iota(jnp.int32, (BLK, BLK), 0)
        r_io = lax.broadcasted_iota(jnp.int32, (BLK, BLK), 1)
        cmask_sc[...] = jnp.where(c_io <= r_io, 0.0, -jnp.inf)
        n_iota = lax.broadcasted_iota(jnp.int32, (nblk, BLK), 0)
        for ib in range(nblk):
            qt_b = (q_ref[ib * BLK:(ib + 1) * BLK, :] * (HD ** -0.5)).T
            qt_sc[ib] = qt_b
            gate2 = jnp.dot(means_sc[...], qt_b, precision=HIGHEST, preferred_element_type=F32)
            for hh in range(2):
                sel_sc[hh, ib] = _topk_select(gate2[hh * nblk:(hh + 1) * nblk, :], ib, n_iota)

    qt = qt_sc[i]
    dim_row = lax.broadcasted_iota(jnp.int32, (PW, BLK), 0)
    qaugs, slopes = [], []
    for hh in range(2):
        slope = slopes_ref[pair * 2 + hh]
        qh = jnp.where((dim_row // HD) == hh, qt, 0.0).astype(BF16)
        slope_row = jnp.where(dim_row == 0, slope, 0.0).astype(BF16)
        qaugs.append(jnp.concatenate([qh, slope_row], axis=0))
        slopes.append(slope)

    def colmax8(s):
        out = s[0:SUBLANES]
        for r in range(1, BLK // SUBLANES):
            out = jnp.maximum(out, s[r * SUBLANES:(r + 1) * SUBLANES])
        return out

    def past_tile(t, hh):
        j = jnp.minimum(t, i - 1)
        on = sel_sc[hh, i, pl.ds(j, 1), :] * jnp.where(t < i, 1.0, 0.0) > 0.0
        delta = -slopes[hh] * ((i - j) * BLK).astype(F32)
        return j, on, delta

    s_own = [_mm(kb_sc[i], qaugs[hh]) + cmask_sc[...] for hh in range(2)]
    n_wide = i // TILES_WIDE
    t_narrow = n_wide * TILES_WIDE
    n_narrow = (i - t_narrow + TILES_NARROW - 1) // TILES_NARROW

    def max_trip(width, t0):
        def trip(tt, mx):
            mx = list(mx)
            for e in range(width):
                for hh in range(2):
                    j, on, delta = past_tile(t0 + width * tt + e, hh)
                    s = _mm(kb_sc[j], qaugs[hh])
                    s_sc[hh, j] = s
                    mx[hh] = jnp.maximum(mx[hh], jnp.where(on, colmax8(s) + delta, -jnp.inf))
            return tuple(mx)
        return trip

    mx = lax.fori_loop(0, n_wide, max_trip(TILES_WIDE, 0), tuple(colmax8(s) for s in s_own))
    mx = lax.fori_loop(0, n_narrow, max_trip(TILES_NARROW, t_narrow), mx)
    m = [jnp.max(x, axis=0, keepdims=True) for x in mx]

    def pv_trip(width, t0):
        def trip(tt, acc):
            acc = list(acc)
            for hh in range(2):
                ps, vts = [], []
                for e in range(width):
                    j, on, delta = past_tile(t0 + width * tt + e, hh)
                    p = jnp.exp(s_sc[hh, j] - jnp.where(on, m[hh] - delta, jnp.inf))
                    ps.append(p.astype(BF16))
                    vts.append(vt_sc[j, hh])
                acc[hh] = acc[hh] + _mm(jnp.concatenate(vts, axis=1), jnp.concatenate(ps, axis=0))
            return tuple(acc)
        return trip

    acc0 = tuple(_mm(vt_sc[i, hh], jnp.exp(s_own[hh] - m[hh]).astype(BF16)) for hh in range(2))
    acc = lax.fori_loop(0, n_wide, pv_trip(TILES_WIDE, 0), acc0)
    acc = lax.fori_loop(0, n_narrow, pv_trip(TILES_NARROW, t_narrow), acc)
    outs = [a[0:HD, :] / a[HD:HD + 1, :] for a in acc]
    o_ref[...] = jnp.concatenate(outs, axis=0).T


PAGES_PER_STEP = 16


def _moba_sample_body(active, ph, st, slope_ref, tq_ref, q_ref, kn_ref, vn_ref, k_refs, v_refs, o_ref,
                      qbd_sc, s_sc, p_sc, acc_sc, linv_sc, *, n_pages, dec_seq):
    PPS = PAGES_PER_STEP
    nsteps = n_pages // PPS
    ppb = MOBA_BLOCK // PAGE_SIZE
    nb = n_pages // ppb
    NR = dec_seq * A_HEADS
    head_of_row = lax.broadcasted_iota(jnp.int32, (A_HEADS, A_WIDTH), 0)
    head_of_lane = lax.broadcasted_iota(jnp.int32, (A_HEADS, A_WIDTH), 1) // A_HEAD_DIM
    hmask = jnp.where(head_of_row == head_of_lane, 1.0, 0.0)

    @pl.when(active & (ph == 0) & (st == 0))
    def _init():
        for t in range(dec_seq):
            qrow = q_ref[t:t + 1, :] * (A_HEAD_DIM ** -0.5)
            qbd_sc[t * A_HEADS:(t + 1) * A_HEADS, :] = qrow * hmask

    @pl.when(active & (ph == 0))
    def _scores():
        qbd = qbd_sc[...].astype(BF16)
        for r in range(PPS):
            s_sc[st * PPS + r] = _mm(qbd, k_refs[r][...].astype(BF16))

    @pl.when(active & (ph == 0) & (st == nsteps - 1))
    def _softmax():
        slope = slope_ref[...]
        tq = tq_ref[...]
        lane = lax.broadcasted_iota(jnp.int32, (NR, PAGE_SIZE), 1)
        lane_f = lane.astype(F32)
        q_of_row = lax.broadcasted_iota(jnp.int32, (NR, PAGE_SIZE), 0) // A_HEADS
        gate = jnp.full((NR, PAGE_SIZE), -jnp.inf, F32)
        for n in range(nb):
            tot = s_sc[ppb * n]
            for e in range(1, ppb):
                tot = tot + s_sc[ppb * n + e]
            gate = jnp.where(lane == n, jnp.sum(tot, axis=-1, keepdims=True) * (1.0 / MOBA_BLOCK), gate)
        sel = jnp.zeros((NR, PAGE_SIZE), F32)
        for _ in range(MOBA_TOPK):
            mx = jnp.max(gate, axis=-1, keepdims=True)
            first = jnp.min(jnp.where(gate == mx, lane_f, float(PAGE_SIZE)), axis=-1, keepdims=True)
            pick = lane_f == first
            sel = jnp.where(pick, 1.0, sel)
            gate = jnp.where(pick, -jnp.inf, gate)
        pad = jnp.zeros((PAGE_SIZE - kn_ref.shape[0], A_WIDTH), F32)
        qbd = qbd_sc[...].astype(BF16)
        s_own = _mm_nt(qbd, jnp.concatenate([kn_ref[...], pad], axis=0).astype(BF16))
        own_ok = (lane <= q_of_row) & (lane < dec_seq)
        s_own = jnp.where(own_ok, s_own - slope * (q_of_row - lane).astype(F32), -jnp.inf)
        m = jnp.max(s_own, axis=-1, keepdims=True)
        for n in range(nb):
            on = jnp.sum(jnp.where(lane == n, sel, 0.0), axis=-1, keepdims=True) > 0.0
            for e in range(ppb):
                pg = ppb * n + e
                sb = s_sc[pg] - slope * (tq - (lane_f + float(pg * PAGE_SIZE)))
                sb = jnp.where(on, sb, -jnp.inf)
                s_sc[pg] = sb
                m = jnp.maximum(m, jnp.max(sb, axis=-1, keepdims=True))
        p_own = jnp.exp(s_own - m)
        l = jnp.sum(p_own, axis=-1, keepdims=True)
        for pg in range(n_pages):
            pb = jnp.exp(s_sc[pg] - m)
            l = l + jnp.sum(pb, axis=-1, keepdims=True)
            p_sc[pg] = pb.astype(BF16)
        linv_sc[...] = jnp.broadcast_to(1.0 / l, (NR, LANES))
        v_own = jnp.concatenate([vn_ref[...], pad], axis=0).astype(BF16)
        acc_sc[...] = _mm(p_own.astype(BF16), v_own)

    @pl.when(active & (ph == 1))
    def _values():
        acc = acc_sc[...]
        for r in range(PPS):
            acc = acc + _mm_nt(p_sc[st * PPS + r], v_refs[r][...].astype(BF16))
        acc_sc[...] = acc

    @pl.when(active & (ph == 1) & (st == nsteps - 1))
    def _finish():
        linv = jnp.concatenate([linv_sc[...]] * (A_WIDTH // LANES), axis=1)
        a = acc_sc[...] * linv
        o_ref[...] = jnp.zeros_like(o_ref)
        for t in range(dec_seq):
            o_ref[t:t + 1, :] = jnp.sum(a[t * A_HEADS:(t + 1) * A_HEADS, :] * hmask, axis=0, keepdims=True)


N_PROMPT_SCRATCH = 7
N_PROMPT_IN = 4
N_SAMPLE_IN = 5


def _sample_item(step, n_items, nsteps):
    w = jnp.minimum(step, n_items - 1)
    return step < n_items, w // (2 * nsteps), (w // nsteps) % 2, w % nsteps


def _moba_kernel(pt_ref, *refs, nblk, n_pages, dec_seq, n_items):
    del pt_ref
    PPS = PAGES_PER_STEP
    prompt_in = refs[:N_PROMPT_IN]
    sample_in = refs[N_PROMPT_IN:N_PROMPT_IN + N_SAMPLE_IN]
    pages = refs[N_PROMPT_IN + N_SAMPLE_IN:N_PROMPT_IN + N_SAMPLE_IN + 2 * PPS]
    o_prompt, o_sample = refs[N_PROMPT_IN + N_SAMPLE_IN + 2 * PPS:N_PROMPT_IN + N_SAMPLE_IN + 2 * PPS + 2]
    scratch = refs[N_PROMPT_IN + N_SAMPLE_IN + 2 * PPS + 2:]
    b, pair, i = pl.program_id(0), pl.program_id(1), pl.program_id(2)
    _moba_prompt_body(pair, i, *prompt_in, o_prompt, *scratch[:N_PROMPT_SCRATCH], nblk=nblk)
    step = (b * pl.num_programs(1) + pair) * nblk + i
    active, _, ph, st = _sample_item(step, n_items, n_pages // PPS)
    _moba_sample_body(active, ph, st, *sample_in, pages[:PPS], pages[PPS:], o_sample,
                      *scratch[N_PROMPT_SCRATCH:], n_pages=n_pages, dec_seq=dec_seq)


def _moba(uf, usf, slopes, slope_rows, tq_rows, ck_t, cv_t, page_table, layer, Bp, L, Bs, tpad, dec_seq):
    nblk = L // MOBA_BLOCK
    PW = 2 * A_HEAD_DIM
    cpb = COL // PW
    npair = A_HEADS // 2
    n_pages = page_table.shape[1]
    PPS = PAGES_PER_STEP
    nsteps = n_pages // PPS
    n_items = Bs * 2 * nsteps
    assert n_items <= Bp * npair * nblk, "more sample page batches than prompt grid steps"
    NR = dec_seq * A_HEADS
    us3 = usf.reshape(Bs, tpad, N_UF)

    def item(b, p, i):
        return _sample_item((b * npair + p) * nblk + i, n_items, nsteps)

    def seq_map(col):
        return lambda b, p, i, pt: (item(b, p, i)[1], 0, col)

    def kmap(r):
        def index(b, p, i, pt):
            _, seq, ph, st = item(b, p, i)
            return layer, pt[seq, (st * (1 - ph) + (nsteps - 1) * ph) * PPS + r], 0, 0
        return index

    def vmap_(r):
        def index(b, p, i, pt):
            _, seq, ph, st = item(b, p, i)
            return layer, pt[seq, (st * ph) * PPS + r], 0, 0
        return index

    page_blk = (None, None, A_WIDTH, PAGE_SIZE)
    in_specs = [pl.BlockSpec(memory_space=pltpu.SMEM),
                pl.BlockSpec((L, PW), lambda b, p, i, pt: (b, FB_QA * cpb + p)),
                pl.BlockSpec((L, PW), lambda b, p, i, pt: (b, FB_KA * cpb + p)),
                pl.BlockSpec((L, PW), lambda b, p, i, pt: (b, FB_VA * cpb + p)),
                pl.BlockSpec((NR, LANES), lambda b, p, i, pt: (0, 0)),
                pl.BlockSpec((NR, LANES), lambda b, p, i, pt: (0, 0)),
                pl.BlockSpec((None, tpad, COL), seq_map(FB_QA)),
                pl.BlockSpec((None, tpad, COL), seq_map(FB_KA)),
                pl.BlockSpec((None, tpad, COL), seq_map(FB_VA))]
    in_specs += [pl.BlockSpec(page_blk, kmap(r)) for r in range(PPS)]
    in_specs += [pl.BlockSpec(page_blk, vmap_(r)) for r in range(PPS)]
    grid_spec = pltpu.PrefetchScalarGridSpec(
        num_scalar_prefetch=1,
        grid=(Bp, npair, nblk),
        in_specs=in_specs,
        out_specs=[pl.BlockSpec((MOBA_BLOCK, PW), lambda b, p, i, pt: (b * nblk + i, p)),
                   pl.BlockSpec((None, tpad, A_WIDTH), seq_map(0))],
        scratch_shapes=[pltpu.VMEM((nblk, MOBA_BLOCK, 2 * PW), BF16),
                        pltpu.VMEM((nblk, 2, A_HEAD_DIM + BF16_ROWS, MOBA_BLOCK), BF16),
                        pltpu.VMEM((2 * nblk, PW), F32),
                        pltpu.VMEM((MOBA_BLOCK, MOBA_BLOCK), F32),
                        pltpu.VMEM((2, nblk, nblk, MOBA_BLOCK), F32),
                        pltpu.VMEM((2, nblk, MOBA_BLOCK, MOBA_BLOCK), F32),
                        pltpu.VMEM((nblk, PW, MOBA_BLOCK), F32),
                        pltpu.VMEM((NR, A_WIDTH), F32),
                        pltpu.VMEM((n_pages, NR, PAGE_SIZE), F32),
                        pltpu.VMEM((n_pages, NR, PAGE_SIZE), BF16),
                        pltpu.VMEM((NR, A_WIDTH), F32),
                        pltpu.VMEM((NR, LANES), F32)])
    oa, oas = pl.pallas_call(
        functools.partial(_moba_kernel, nblk=nblk, n_pages=n_pages, dec_seq=dec_seq, n_items=n_items),
        grid_spec=grid_spec,
        out_shape=[jax.ShapeDtypeStruct((Bp * L, A_WIDTH), F32), jax.ShapeDtypeStruct((Bs, tpad, A_WIDTH), F32)],
        compiler_params=_cparams("arbitrary", "arbitrary", "arbitrary"),
        name="moba",
    )(page_table, slopes, uf, uf, uf, slope_rows, tq_rows, us3, us3, us3, *([ck_t] * PPS), *([cv_t] * PPS))
    return oa, oas.reshape(Bs * tpad, A_WIDTH)


def _lru_kernel(xb_ref, zb_ref, conv0_ref, h0_ref, cw_ref, cb_ref, wa_ref, ba_ref, wx_ref, bx_ref, lam_ref,
                y_ref, conv1_ref, h1_ref, xp_sc, h_sc, a_sc, b_sc, *, Bn, tc, l_valid):
    c = pl.program_id(0)

    @pl.when(c == 0)
    def _():
        xp_sc[:, TAIL0:SUBLANES, :] = conv0_ref[...]
        h_sc[...] = h0_ref[...]

    xp_sc[:, SUBLANES:SUBLANES + tc, :] = xb_ref[...].astype(F32)
    xc = xp_sc[:, TAIL0:TAIL0 + tc, :] * cw_ref[0:1, :]
    for j in range(1, CONV_WIDTH):
        xc = xc + xp_sc[:, TAIL0 + j:TAIL0 + j + tc, :] * cw_ref[j:j + 1, :]
    xc = xc + cb_ref[...]
    nv = min(tc, l_valid)
    tail = xp_sc[:, TAIL0 + nv:SUBLANES + nv, :]
    conv1_ref[...] = tail
    xp_sc[:, TAIL0:SUBLANES, :] = tail

    x2 = xc.reshape(Bn * tc, B_WIDTH)
    x16 = x2.astype(BF16)
    r = _sigmoid(jnp.dot(x16, wa_ref[...], preferred_element_type=F32) + ba_ref[...])
    ig = _sigmoid(jnp.dot(x16, wx_ref[...], preferred_element_type=F32) + bx_ref[...])
    log_a = -LRU_C * r * _softplus(-lam_ref[...])
    a = jnp.exp(log_a)
    bin_ = jnp.sqrt(-jnp.tanh(log_a) * (a * a + 1.0)) * (ig * x2)
    a_sc[...] = a.reshape(Bn, tc, B_WIDTH)
    b_sc[...] = bin_.reshape(Bn, tc, B_WIDTH)

    def step(t, h):
        h = a_sc[:, pl.ds(t, 1), :] * h + b_sc[:, pl.ds(t, 1), :]
        a_sc[:, pl.ds(t, 1), :] = h
        return h

    h = lax.fori_loop(0, nv, step, h_sc[...])
    h_sc[...] = h
    h1_ref[...] = h
    y_ref[...] = _silu(zb_ref[...].astype(F32)) * a_sc[...]


def _lru(u3, conv0, h0, cw, cb, wa_bd, ba, wx_bd, bx, lam, tc, l_valid):
    Bn, Lp, _ = u3.shape
    row = lambda v: v.reshape(1, B_WIDTH)
    const2 = lambda c: (0, 0)
    return pl.pallas_call(
        functools.partial(_lru_kernel, Bn=Bn, tc=tc, l_valid=l_valid),
        grid=(Lp // tc,),
        in_specs=[pl.BlockSpec((Bn, tc, COL), lambda c: (0, c, HB_XB)),
                  pl.BlockSpec((Bn, tc, COL), lambda c: (0, c, HB_ZB)),
                  pl.BlockSpec((Bn, CONV_WIDTH - 1, B_WIDTH), lambda c: (0, 0, 0)),
                  pl.BlockSpec((Bn, 1, B_WIDTH), lambda c: (0, 0, 0)),
                  pl.BlockSpec((CONV_WIDTH, B_WIDTH), const2),
                  pl.BlockSpec((1, B_WIDTH), const2),
                  pl.BlockSpec((B_WIDTH, B_WIDTH), const2),
                  pl.BlockSpec((1, B_WIDTH), const2),
                  pl.BlockSpec((B_WIDTH, B_WIDTH), const2),
                  pl.BlockSpec((1, B_WIDTH), const2),
                  pl.BlockSpec((1, B_WIDTH), const2)],
        out_specs=[pl.BlockSpec((Bn, tc, B_WIDTH), lambda c: (0, c, 0)),
                   pl.BlockSpec((Bn, CONV_WIDTH - 1, B_WIDTH), lambda c: (0, 0, 0)),
                   pl.BlockSpec((Bn, 1, B_WIDTH), lambda c: (0, 0, 0))],
        out_shape=[jax.ShapeDtypeStruct((Bn, Lp, B_WIDTH), F32),
                   jax.ShapeDtypeStruct((Bn, CONV_WIDTH - 1, B_WIDTH), F32),
                   jax.ShapeDtypeStruct((Bn, 1, B_WIDTH), F32)],
        scratch_shapes=[pltpu.VMEM((Bn, SUBLANES + tc, B_WIDTH), F32),
                        pltpu.VMEM((Bn, 1, B_WIDTH), F32),
                        pltpu.VMEM((Bn, tc, B_WIDTH), F32),
                        pltpu.VMEM((Bn, tc, B_WIDTH), F32)],
        compiler_params=_cparams("arbitrary"),
        name="rglru",
    )(u3, u3, conv0, h0.reshape(Bn, 1, B_WIDTH), cw, row(cb), wa_bd, row(ba), wx_bd, row(bx), row(lam))


def _gdn_kernel(qkv_ref, beta_ref, alpha_ref, z_ref, conv0_ref, s0_ref, cw_ref, alog_ref, dtb_ref, nw_ref,
                o_ref, conv1_ref, s1_ref, xp_sc, s_sc, y_sc, *, tb, l_valid):
    C = DELTA_CHUNK
    R = max(tb, C)
    nc = R // C
    t = pl.program_id(1)

    @pl.when(t == 0)
    def _():
        xp_sc[TAIL0:SUBLANES, :] = conv0_ref[...]
        s_sc[...] = s0_ref[...]

    xp_sc[SUBLANES:SUBLANES + tb, :] = qkv_ref[...].astype(F32)
    y = xp_sc[TAIL0:TAIL0 + tb, :] * cw_ref[0:1, :]
    for j in range(1, CONV_WIDTH):
        y = y + xp_sc[TAIL0 + j:TAIL0 + j + tb, :] * cw_ref[j:j + 1, :]
    nv = min(tb, l_valid)
    tail = xp_sc[TAIL0 + nv:SUBLANES + nv, :]
    conv1_ref[...] = tail
    xp_sc[TAIL0:SUBLANES, :] = tail
    y = _silu(y)

    if nv < R:
        live = lax.broadcasted_iota(jnp.int32, (tb, 1), 0) < nv
        y_sc[...] = jnp.zeros_like(y_sc)
        y_sc[0:tb, :] = jnp.where(live, y, 0.0)
    else:
        live = None
        y_sc[...] = y

    ri = lax.broadcasted_iota(jnp.int32, (R, R), 0)
    ci = lax.broadcasted_iota(jnp.int32, (R, R), 1)
    same = (ri // C) == (ci // C)
    causal = same & (ri >= ci)
    strict = same & (ri > ci)
    diag = ri == ci
    eye = jnp.where(diag, 1.0, 0.0)
    ltri16 = jnp.where(causal, 1.0, 0.0).astype(BF16)

    def padrows(v):
        if tb == R:
            return v
        return jnp.concatenate([v, jnp.zeros((R - tb, v.shape[1]), F32)], axis=0)

    def widen(v):
        return v[:, 0:R] if R <= LANES else jnp.concatenate([v] * (R // LANES), axis=1)

    heads = range(C_HEADS)
    hsl = [slice(h * C_KEY_DIM, (h + 1) * C_KEY_DIM) for h in heads]
    st = []
    for h in heads:
        q = y_sc[:, h * C_KEY_DIM:(h + 1) * C_KEY_DIM]
        k = y_sc[:, C_QK_WIDTH + h * C_KEY_DIM:C_QK_WIDTH + (h + 1) * C_KEY_DIM]
        v = y_sc[:, 2 * C_QK_WIDTH + h * C_VAL_DIM:2 * C_QK_WIDTH + (h + 1) * C_VAL_DIM]
        q = q * lax.rsqrt(jnp.sum(q * q, axis=-1, keepdims=True) + EPS) * (C_KEY_DIM ** -0.5)
        k = k * lax.rsqrt(jnp.sum(k * k, axis=-1, keepdims=True) + EPS)
        beta = _sigmoid(beta_ref[:, hsl[h]])
        g = -jnp.exp(alog_ref[:, hsl[h]]) * _softplus(alpha_ref[:, hsl[h]] + dtb_ref[:, hsl[h]])
        if live is not None:
            beta = jnp.where(live, beta, 0.0)
            g = jnp.where(live, g, 0.0)
        beta = padrows(beta)
        g = padrows(g)
        g_hi, g_lo = _split2(g)
        g_lo2 = (g - g_hi.astype(F32) - g_lo.astype(F32)).astype(BF16)
        G = _mm(ltri16, g_hi) + (_mm(ltri16, g_lo) + _mm(ltri16, g_lo2))
        Gcol = widen(G)
        Grow = jnp.sum(jnp.where(diag, Gcol, 0.0), axis=0, keepdims=True)
        decay = jnp.exp(jnp.where(causal, Gcol - Grow, -jnp.inf))
        kb = k * beta
        k16 = k.astype(BF16)
        A = jnp.where(strict, _mm_nt(kb.astype(BF16), k16) * decay, 0.0)
        eG = jnp.exp(G)
        rhs16 = jnp.concatenate([v * beta, kb * eG], axis=1).astype(BF16)
        QK16 = (_mm_nt(q.astype(BF16), k16) * decay).astype(BF16)
        glast = jnp.concatenate([jnp.broadcast_to(G[(c + 1) * C - 1:(c + 1) * C, :], (C, LANES)) for c in range(nc)],
                                axis=0)
        st.append(dict(M=-A, rhs16=rhs16, QK16=QK16, qg16=(q * eG).astype(BF16),
                       kg16=(k * jnp.exp(glast - G)).astype(BF16),
                       gl=[jnp.exp(G[(c + 1) * C - 1:(c + 1) * C, :]) for c in range(nc)]))

    for s in st:
        s["T"] = eye + s["M"]
        s["Ms"] = _split2(s["M"])
    for it in range(5):
        if it < NEUMANN_SPLIT_STEPS:
            for s in st:
                s["M"] = _mm_split(s["Ms"], s["Ms"])
            for s in st:
                s["Ms"] = _split2(s["M"])
                s["T"] = s["T"] + _mm_split(_split2(s["T"]), s["Ms"])
        else:
            for s in st:
                s["M"] = _mm(s["Ms"][0], s["Ms"][0])
            for s in st:
                s["Ms"] = (s["M"].astype(BF16),)
                s["T"] = s["T"] + _mm(s["T"].astype(BF16), s["Ms"][0])
    for h, s in zip(heads, st):
        UW = _mm(s["T"].astype(BF16), s["rhs16"])
        s["U"] = UW[:, 0:C_VAL_DIM]
        s["W16"] = UW[:, C_VAL_DIM:].astype(BF16)
        s["S"] = s_sc[h]
        s["vn"], s["oi"] = [], []

    for c in range(nc):
        rs = slice(c * C, (c + 1) * C)
        for s in st:
            S16 = s["S"].astype(BF16)
            vn = s["U"][rs] - _mm(s["W16"][rs], S16)
            s["oi"].append(_mm(s["qg16"][rs], S16))
            s["S"] = s["S"] * s["gl"][c] + lax.dot_general(s["kg16"][rs], vn.astype(BF16), TN_DIMS,
                                                           preferred_element_type=F32)
            s["vn"].append(vn)
    for h, s in zip(heads, st):
        s_sc[h] = s["S"]
        s1_ref[h] = s["S"]
        o = jnp.concatenate(s["oi"], axis=0) + _mm(s["QK16"], jnp.concatenate(s["vn"], axis=0).astype(BF16))
        o = o * lax.rsqrt(jnp.mean(o * o, axis=-1, keepdims=True) + EPS) * nw_ref[...]
        o_ref[:, hsl[h]] = o[0:tb] * _silu(z_ref[:, hsl[h]].astype(F32))


def _gdn(uf3, uh3, conv0, s0, cw, alog_e, dtb_e, nw, tb, l_valid):
    Bn, Lp, _ = uf3.shape
    rows = max(tb, DELTA_CHUNK)
    const2 = lambda b, t: (0, 0)
    return pl.pallas_call(
        functools.partial(_gdn_kernel, tb=tb, l_valid=l_valid),
        grid=(Bn, Lp // tb),
        in_specs=[pl.BlockSpec((None, tb, C_CONV_DIM), lambda b, t: (b, t, HB_QKVC * COL // C_CONV_DIM)),
                  pl.BlockSpec((None, tb, COL), lambda b, t: (b, t, FB_BETA)),
                  pl.BlockSpec((None, tb, COL), lambda b, t: (b, t, FB_ALPHA)),
                  pl.BlockSpec((None, tb, COL), lambda b, t: (b, t, HB_ZC)),
                  pl.BlockSpec((None, CONV_WIDTH - 1, C_CONV_DIM), lambda b, t: (b, 0, 0)),
                  pl.BlockSpec((None, C_HEADS, C_KEY_DIM, C_VAL_DIM), lambda b, t: (b, 0, 0, 0)),
                  pl.BlockSpec((CONV_WIDTH, C_CONV_DIM), const2),
                  pl.BlockSpec((1, C_V_WIDTH), const2),
                  pl.BlockSpec((1, C_V_WIDTH), const2),
                  pl.BlockSpec((1, C_VAL_DIM), const2)],
        out_specs=[pl.BlockSpec((None, tb, C_V_WIDTH), lambda b, t: (b, t, 0)),
                   pl.BlockSpec((None, CONV_WIDTH - 1, C_CONV_DIM), lambda b, t: (b, 0, 0)),
                   pl.BlockSpec((None, C_HEADS, C_KEY_DIM, C_VAL_DIM), lambda b, t: (b, 0, 0, 0))],
        out_shape=[jax.ShapeDtypeStruct((Bn, Lp, C_V_WIDTH), F32),
                   jax.ShapeDtypeStruct((Bn, CONV_WIDTH - 1, C_CONV_DIM), F32),
                   jax.ShapeDtypeStruct((Bn, C_HEADS, C_KEY_DIM, C_VAL_DIM), F32)],
        scratch_shapes=[pltpu.VMEM((SUBLANES + tb, C_CONV_DIM), F32),
                        pltpu.VMEM((C_HEADS, C_KEY_DIM, C_VAL_DIM), F32),
                        pltpu.VMEM((rows, C_CONV_DIM), F32)],
        compiler_params=_cparams("parallel", "arbitrary"),
        name="gdn",
    )(uh3, uf3, uf3, uh3, conv0, s0, cw, alog_e, dtb_e, nw.reshape(1, C_VAL_DIM))


def _merge_kernel(h_ref, oa_ref, za_ref, ob_ref, oc_ref, g0_ref, g1_ref, g2_ref,
                  wpa_ref, wpb_ref, wpc_ref, wout_ref, fnw_ref, hn_ref, *, final_norm):
    def proj(x, w_ref):
        return jnp.dot(x.astype(BF16), w_ref[...], preferred_element_type=F32)

    f32 = lambda ref: ref[...].astype(F32)
    ya = proj(_silu(f32(za_ref)) * oa_ref[...], wpa_ref)
    yb = proj(ob_ref[...], wpb_ref)
    yc = proj(oc_ref[...], wpc_ref)
    merged = _sigmoid(f32(g0_ref)) * ya + _sigmoid(f32(g1_ref)) * yb + _sigmoid(f32(g2_ref)) * yc
    hn = h_ref[...] + proj(merged, wout_ref)
    if final_norm:
        hn = hn * lax.rsqrt(jnp.mean(hn * hn, axis=-1, keepdims=True) + EPS) * fnw_ref[...]
    hn_ref[...] = hn


def _merge(h2d, oa, uh, ob, oc, wpa, wpb, wpc, wout, fnw, tm, final_norm):
    T = h2d.shape[0]
    gpb = D_MODEL // COL
    tok = lambda w: pl.BlockSpec((tm, w), lambda i: (i, 0))
    wspec = lambda k: pl.BlockSpec((k, D_MODEL), lambda i: (0, 0))
    gate_spec = lambda g: pl.BlockSpec((tm, D_MODEL), lambda i: (i, HB_GATES // gpb + g))
    return pl.pallas_call(
        functools.partial(_merge_kernel, final_norm=final_norm),
        grid=(T // tm,),
        in_specs=[tok(D_MODEL), tok(A_WIDTH),
                  pl.BlockSpec((tm, COL), lambda i: (i, HB_ZA)),
                  tok(B_WIDTH), tok(C_V_WIDTH),
                  gate_spec(0), gate_spec(1), gate_spec(2),
                  wspec(A_WIDTH), wspec(B_WIDTH), wspec(C_V_WIDTH), wspec(D_MODEL),
                  pl.BlockSpec((1, D_MODEL), lambda i: (0, 0))],
        out_specs=tok(D_MODEL),
        out_shape=jax.ShapeDtypeStruct((T, D_MODEL), F32),
        compiler_params=_cparams("parallel"),
        name="merge",
    )(h2d, oa, uh, ob, oc, uh, uh, uh, wpa, wpb, wpc, wout, fnw.reshape(1, D_MODEL))


def _block_diag(w):
    g, n, _ = w.shape
    out = jnp.zeros((g * n, g * n), w.dtype)
    for i in range(g):
        out = out.at[i * n:(i + 1) * n, i * n:(i + 1) * n].set(w[i])
    return out


def _rearrange_w_in(w):
    sizes = (A_WIDTH, A_WIDTH, A_WIDTH, A_WIDTH, B_WIDTH, B_WIDTH, C_CONV_DIM, C_V_WIDTH, C_HEADS, C_HEADS,
             N_BRANCH * D_MODEL)
    offs = [0]
    for s in sizes:
        offs.append(offs[-1] + s)
    qa, ka, va, za, xb, zb, qkvc, zc, beta, alpha, gates = (w[:, offs[n]:offs[n + 1]] for n in range(len(sizes)))
    rep = lambda c: jnp.repeat(c, C_VAL_DIM, axis=1)
    return jnp.concatenate([qa, ka, va, rep(beta), rep(alpha), qkvc, za, gates, xb, zb, zc], axis=1).astype(BF16)


def kernel(x_prompt, x_sample, cache_k, cache_v, page_table, state_conv_b, state_h_b, state_conv_c, state_s_c,
           norm_w, w_in, conv_b_w, conv_b_b, lru_wa, lru_ba, lru_wx, lru_bx, lru_lambda,
           conv_c_w, dn_a_log, dn_dt_bias, dn_norm_w, w_pa, w_pb, w_pc, w_out, final_norm_w):
    Bp, L, _ = x_prompt.shape
    Bs, dec_seq, _ = x_sample.shape
    depth = w_in.shape[0]
    n_pool = cache_k.shape[1]
    n_pages = page_table.shape[1]
    past = n_pages * PAGE_SIZE
    tpad = BF16_ROWS
    assert dec_seq <= tpad and L % MOBA_BLOCK == 0 and past % MOBA_BLOCK == 0 and n_pages % PAGES_PER_STEP == 0
    assert MOBA_TOPK <= past // MOBA_BLOCK <= PAGE_SIZE and (dec_seq * A_HEADS) % BF16_ROWS == 0

    slopes = jnp.exp2(-8.0 * jnp.arange(1, A_HEADS + 1, dtype=F32) / A_HEADS)
    row = jnp.arange(dec_seq * A_HEADS)
    slope_rows = jnp.broadcast_to(slopes[row % A_HEADS][:, None], (row.size, LANES))
    tq_rows = jnp.broadcast_to((past + row // A_HEADS).astype(F32)[:, None], (row.size, LANES))
    ck_t = jnp.transpose(cache_k, (0, 1, 3, 4, 2)).reshape(depth, n_pool, A_WIDTH, PAGE_SIZE)
    cv_t = jnp.transpose(cache_v, (0, 1, 3, 4, 2)).reshape(depth, n_pool, A_WIDTH, PAGE_SIZE)

    hp = x_prompt.reshape(Bp * L, D_MODEL)
    hs = jnp.pad(x_sample, ((0, 0), (0, tpad - dec_seq), (0, 0))).reshape(Bs * tpad, D_MODEL)
    zeros = lambda *s: jnp.zeros(s, F32)
    tm_p = 2048 if (Bp * L) % 2048 == 0 else MOBA_BLOCK
    tm_merge = min(tm_p, 512)
    tm_s = Bs * tpad
    tc_p = 256 if L % 256 == 0 else L
    outs = [[] for _ in range(12)]
    for l in range(depth):
        w_re = _rearrange_w_in(w_in[l])
        wa_bd = _block_diag(lru_wa[l]).astype(BF16)
        wx_bd = _block_diag(lru_wx[l]).astype(BF16)
        alog_e = jnp.repeat(dn_a_log[l], C_VAL_DIM).reshape(1, C_V_WIDTH)
        dtb_e = jnp.repeat(dn_dt_bias[l], C_VAL_DIM).reshape(1, C_V_WIDTH)
        wpa, wpb, wpc, wout = (w.astype(BF16) for w in (w_pa[l], w_pb[l], w_pc[l], w_out[l]))
        last = l == depth - 1
        lru_w = (conv_b_w[l], conv_b_b[l], wa_bd, lru_ba[l], wx_bd, lru_bx[l], lru_lambda[l])
        gdn_w = (conv_c_w[l], alog_e, dtb_e, dn_norm_w[l])

        uf, uh = _inproj(hp, norm_w[l], w_re, tm_p)
        uf3 = uf.reshape(Bp, L, N_UF)
        uh3 = uh.reshape(Bp, L, N_UH)
        usf, ush = _inproj(hs, norm_w[l], w_re, tm_s)
        oa, oas = _moba(uf, usf, slopes, slope_rows, tq_rows, ck_t, cv_t, page_table, l, Bp, L, Bs, tpad, dec_seq)
        ob, cbp, hbp = _lru(uh3, zeros(Bp, CONV_WIDTH - 1, B_WIDTH), zeros(Bp, B_WIDTH), *lru_w, tc_p, L)
        oc, ccp, scp = _gdn(uf3, uh3, zeros(Bp, CONV_WIDTH - 1, C_CONV_DIM),
                            zeros(Bp, C_HEADS, C_KEY_DIM, C_VAL_DIM), *gdn_w, tc_p, L)
        hp = _merge(hp, oa, uh, ob.reshape(Bp * L, B_WIDTH), oc.reshape(Bp * L, C_V_WIDTH),
                    wpa, wpb, wpc, wout, final_norm_w, tm_merge, last)
        kp = uf3[:, :, FB_KA * COL:(FB_KA + 1) * COL].reshape(Bp, L, A_HEADS, A_HEAD_DIM)
        vp = uf3[:, :, FB_VA * COL:(FB_VA + 1) * COL].reshape(Bp, L, A_HEADS, A_HEAD_DIM)

        usf3 = usf.reshape(Bs, tpad, N_UF)
        ush3 = ush.reshape(Bs, tpad, N_UH)
        obs, cbs, hbs = _lru(ush3, state_conv_b[l], state_h_b[l], *lru_w, tpad, dec_seq)
        ocs, ccs, scs = _gdn(usf3, ush3, state_conv_c[l], state_s_c[l], *gdn_w, tpad, dec_seq)
        hs = _merge(hs, oas, ush, obs.reshape(Bs * tpad, B_WIDTH), ocs.reshape(Bs * tpad, C_V_WIDTH),
                    wpa, wpb, wpc, wout, final_norm_w, tm_s, last)
        ks = usf3[:, :dec_seq, FB_KA * COL:(FB_KA + 1) * COL].reshape(Bs, dec_seq, A_HEADS, A_HEAD_DIM)
        vs = usf3[:, :dec_seq, FB_VA * COL:(FB_VA + 1) * COL].reshape(Bs, dec_seq, A_HEADS, A_HEAD_DIM)

        hbp = hbp.reshape(Bp, B_WIDTH)
        hbs = hbs.reshape(Bs, B_WIDTH)
        for lst, val in zip(outs, (kp, vp, ks, vs, cbp, hbp, cbs, hbs, ccp, scp, ccs, scs)):
            lst.append(val)

    y_prompt = hp.reshape(Bp, L, D_MODEL)
    y_sample = hs.reshape(Bs, tpad, D_MODEL)[:, :dec_seq]
    return (y_prompt, y_sample) + tuple(jnp.stack(o) for o in outs)
```

```python
import functools

import jax
import jax.numpy as jnp
from jax import lax
from jax.experimental import pallas as pl
from jax.experimental.pallas import tpu as pltpu

F32 = jnp.float32
BF16 = jnp.bfloat16
HIGHEST = lax.Precision.HIGHEST

D_MODEL = 1024
A_HEADS = 8
A_HEAD_DIM = 64
A_WIDTH = A_HEADS * A_HEAD_DIM
MOBA_BLOCK = 256
MOBA_TOPK = 3
PAGE_SIZE = 128
B_WIDTH = 512
B_BLOCKS = 8
LRU_C = 8.0
CONV_WIDTH = 4
C_HEADS = 4
C_KEY_DIM = 128
C_VAL_DIM = 128
C_QK_WIDTH = C_HEADS * C_KEY_DIM
C_V_WIDTH = C_HEADS * C_VAL_DIM
C_CONV_DIM = 2 * C_QK_WIDTH + C_V_WIDTH
DELTA_CHUNK = 64
NEUMANN_SPLIT_STEPS = 2
N_BRANCH = 3
EPS = 1e-6

COL = 512
STEP_QA, STEP_KA, STEP_VA, STEP_BETA, STEP_ALPHA, N_FSTEP = 0, 1, 2, 3, 4, 5
FB_QA, FB_BETA, FB_ALPHA = 0, 1, 2
N_FBLK = 3
HB_QKVC, HB_ZA, HB_GATES, HB_XB, HB_ZB, HB_ZC = 0, 3, 4, 10, 11, 12
N_HBLK = 13
N_UF = N_FBLK * COL
N_UH = N_HBLK * COL

SUBLANES = 8
LANES = 128
BF16_ROWS = 16
TAIL0 = SUBLANES - (CONV_WIDTH - 1)
VMEM_LIMIT = 56 * 1024 * 1024

NT_DIMS = (((1,), (1,)), ((), ()))
TN_DIMS = (((0,), (0,)), ((), ()))


def _cparams(*sem):
    return pltpu.CompilerParams(dimension_semantics=sem, vmem_limit_bytes=VMEM_LIMIT)


def _sigmoid(x):
    return 0.5 * jnp.tanh(0.5 * x) + 0.5


def _silu(x):
    return x * _sigmoid(x)


def _softplus(x):
    return jnp.maximum(x, 0.0) + jnp.log1p(jnp.exp(-jnp.abs(x)))


def _mm(a, b):
    return jnp.dot(a, b, preferred_element_type=F32)


def _mm_nt(a, b):
    return lax.dot_general(a, b, NT_DIMS, preferred_element_type=F32)


def _split2(x):
    hi = x.astype(BF16)
    lo = (x - hi.astype(F32)).astype(BF16)
    return hi, lo


def _mm_split(a, b):
    (ah, al), (bh, bl) = a, b
    return _mm(ah, bh) + (_mm(ah, bl) + _mm(al, bh))


def _inproj_kernel(h_ref, nw_ref, w_ref, uf_ref, k_ref, v_ref, uh_ref, xn_sc):
    j = pl.program_id(1)

    @pl.when(j == 0)
    def _():
        x = h_ref[...]
        y = x * lax.rsqrt(jnp.mean(x * x, axis=-1, keepdims=True) + EPS) * nw_ref[...]
        xn_sc[...] = y.astype(BF16)

    def block():
        return jnp.dot(xn_sc[...], w_ref[...], preferred_element_type=F32)

    @pl.when((j == STEP_QA) | (j == STEP_BETA) | (j == STEP_ALPHA))
    def _():
        uf_ref[...] = block()

    @pl.when(j == STEP_KA)
    def _():
        k_ref[...] = block()

    @pl.when(j == STEP_VA)
    def _():
        v_ref[...] = block()

    @pl.when(j >= N_FSTEP)
    def _():
        uh_ref[...] = block().astype(BF16)


def _inproj(h2d, norm_w, w_re, tm):
    T = h2d.shape[0]
    return pl.pallas_call(
        _inproj_kernel,
        grid=(T // tm, N_FSTEP + N_HBLK),
        in_specs=[pl.BlockSpec((tm, D_MODEL), lambda i, j: (i, 0)),
                  pl.BlockSpec((1, D_MODEL), lambda i, j: (0, 0)),
                  pl.BlockSpec((D_MODEL, COL), lambda i, j: (0, j))],
        out_specs=[pl.BlockSpec((tm, COL), lambda i, j: (i, jnp.clip(j - (STEP_BETA - FB_BETA), 0, N_FBLK - 1))),
                   pl.BlockSpec((tm, COL), lambda i, j: (i, 0)),
                   pl.BlockSpec((tm, COL), lambda i, j: (i, 0)),
                   pl.BlockSpec((tm, COL), lambda i, j: (i, jnp.maximum(j - N_FSTEP, 0)))],
        out_shape=[jax.ShapeDtypeStruct((T, N_UF), F32), jax.ShapeDtypeStruct((T, A_WIDTH), F32),
                   jax.ShapeDtypeStruct((T, A_WIDTH), F32), jax.ShapeDtypeStruct((T, N_UH), BF16)],
        scratch_shapes=[pltpu.VMEM((tm, D_MODEL), BF16)],
        compiler_params=_cparams("parallel", "arbitrary"),
        name="inproj",
    )(h2d, norm_w.reshape(1, D_MODEL), w_re)


TILES_WIDE = 4
TILES_NARROW = 2

def _topk_select(gate, n_past, n_iota):
    nb = gate.shape[0]
    g = jnp.where(n_iota < n_past, gate, -jnp.inf)
    rank = jnp.zeros_like(g)
    for m in range(min(nb, n_past) if isinstance(n_past, int) else nb):
        gm = g[m:m + 1, :]
        rank = rank + jnp.where(n_iota > m, jnp.where(gm >= g, 1.0, 0.0), jnp.where(gm > g, 1.0, 0.0))
    return jnp.where(n_iota < n_past, jnp.where(rank < MOBA_TOPK, 1.0, 0.0), 0.0)


def _moba_prompt_body(pair, i, slopes_ref, q_ref, k_ref, v_ref, o_ref, kb_sc, vt_sc, means_sc, cmask_sc, sel_sc, s_sc,
                      qt_sc, *, nblk):
    BLK, HD, PW = MOBA_BLOCK, A_HEAD_DIM, 2 * A_HEAD_DIM
    lane = lax.broadcasted_iota(jnp.int32, (1, PW), 1)

    @pl.when(i == 0)
    def _init():
        c_col = lax.broadcasted_iota(jnp.int32, (BLK, PW), 0).astype(F32)
        lane_a = lax.broadcasted_iota(jnp.int32, (BLK, PW), 1)
        key_off = jnp.where(lane_a == 0, c_col, 0.0).astype(BF16)
        ones = jnp.ones((BF16_ROWS, BLK), BF16)
        for n in range(nblk):
            kblk = k_ref[n * BLK:(n + 1) * BLK, :]
            mean = jnp.sum(kblk, axis=0, keepdims=True) * (1.0 / BLK)
            means_sc[n:n + 1, :] = jnp.where(lane < HD, mean, 0.0)
            means_sc[nblk + n:nblk + n + 1, :] = jnp.where(lane < HD, 0.0, mean)
            kb_sc[n, :, 0:PW] = kblk.astype(BF16)
            kb_sc[n, :, PW:2 * PW] = key_off
            vt = v_ref[n * BLK:(n + 1) * BLK, :].T
            for hh in range(2):
                vt_sc[n, hh, 0:HD, :] = vt[hh * HD:(hh + 1) * HD, :].astype(BF16)
                vt_sc[n, hh, HD:HD + BF16_ROWS, :] = ones
        c_io = lax.broadcasted_iota(jnp.int32, (BLK, BLK), 0)
        r_io = lax.broadcasted_iota(jnp.int32, (BLK, BLK), 1)
        cmask_sc[...] = jnp.where(c_io <= r_io, 0.0, -jnp.inf)
        n_iota = lax.broadcasted_iota(jnp.int32, (nblk, BLK), 0)
        for ib in range(nblk):
            qt_b = (q_ref[ib * BLK:(ib + 1) * BLK, :] * (HD ** -0.5)).T
            qt_sc[ib] = qt_b
            gate2 = jnp.dot(means_sc[...], qt_b, precision=HIGHEST, preferred_element_type=F32)
            for hh in range(2):
                sel_sc[hh, ib] = _topk_select(gate2[hh * nblk:(hh + 1) * nblk, :], ib, n_iota)

    qt = qt_sc[i]
    dim_row = lax.broadcasted_iota(jnp.int32, (PW, BLK), 0)
    qaugs, slopes = [], []
    for hh in range(2):
        slope = slopes_ref[pair * 2 + hh]
        qh = jnp.where((dim_row // HD) == hh, qt, 0.0).astype(BF16)
        slope_row = jnp.where(dim_row == 0, slope, 0.0).astype(BF16)
        qaugs.append(jnp.concatenate([qh, slope_row], axis=0))
        slopes.append(slope)

    def colmax8(s):
        out = s[0:SUBLANES]
        for r in range(1, BLK // SUBLANES):
            out = jnp.maximum(out, s[r * SUBLANES:(r + 1) * SUBLANES])
        return out

    def past_tile(t, hh):
        j = jnp.minimum(t, i - 1)
        on = sel_sc[hh, i, pl.ds(j, 1), :] * jnp.where(t < i, 1.0, 0.0) > 0.0
        delta = -slopes[hh] * ((i - j) * BLK).astype(F32)
        return j, on, delta

    s_own = [_mm(kb_sc[i], qaugs[hh]) + cmask_sc[...] for hh in range(2)]
    n_wide = i // TILES_WIDE
    t_narrow = n_wide * TILES_WIDE
    n_narrow = (i - t_narrow + TILES_NARROW - 1) // TILES_NARROW

    def max_trip(width, t0):
        def trip(tt, mx):
            mx = list(mx)
            for e in range(width):
                for hh in range(2):
                    j, on, delta = past_tile(t0 + width * tt + e, hh)
                    s = _mm(kb_sc[j], qaugs[hh])
                    s_sc[hh, j] = s
                    mx[hh] = jnp.maximum(mx[hh], jnp.where(on, colmax8(s) + delta, -jnp.inf))
            return tuple(mx)
        return trip

    mx = lax.fori_loop(0, n_wide, max_trip(TILES_WIDE, 0), tuple(colmax8(s) for s in s_own))
    mx = lax.fori_loop(0, n_narrow, max_trip(TILES_NARROW, t_narrow), mx)
    m = [jnp.max(x, axis=0, keepdims=True) for x in mx]

    def pv_trip(width, t0):
        def trip(tt, acc):
            acc = list(acc)
            for hh in range(2):
                ps, vts = [], []
                for e in range(width):
                    j, on, delta = past_tile(t0 + width * tt + e, hh)
                    p = jnp.exp(s_sc[hh, j] - jnp.where(on, m[hh] - delta, jnp.inf))
                    ps.append(p.astype(BF16))
                    vts.append(vt_sc[j, hh])
                acc[hh] = acc[hh] + _mm(jnp.concatenate(vts, axis=1), jnp.concatenate(ps, axis=0))
            return tuple(acc)
        return trip

    acc0 = tuple(_mm(vt_sc[i, hh], jnp.exp(s_own[hh] - m[hh]).astype(BF16)) for hh in range(2))
    acc = lax.fori_loop(0, n_wide, pv_trip(TILES_WIDE, 0), acc0)
    acc = lax.fori_loop(0, n_narrow, pv_trip(TILES_NARROW, t_narrow), acc)
    outs = [a[0:HD, :] / a[HD:HD + 1, :] for a in acc]
    o_ref[...] = jnp.concatenate(outs, axis=0).T


PAGES_PER_STEP = 16


def _moba_sample_body(active, ph, st, slope_ref, tq_ref, q_ref, kn_ref, vn_ref, k_refs, v_refs, o_ref,
                      qbd_sc, s_sc, p_sc, acc_sc, linv_sc, *, n_pages, dec_seq):
    PPS = PAGES_PER_STEP
    nsteps = n_pages // PPS
    ppb = MOBA_BLOCK // PAGE_SIZE
    nb = n_pages // ppb
    NR = dec_seq * A_HEADS
    head_of_row = lax.broadcasted_iota(jnp.int32, (A_HEADS, A_WIDTH), 0)
    head_of_lane = lax.broadcasted_iota(jnp.int32, (A_HEADS, A_WIDTH), 1) // A_HEAD_DIM
    hmask = jnp.where(head_of_row == head_of_lane, 1.0, 0.0)

    @pl.when(active & (ph == 0) & (st == 0))
    def _init():
        for t in range(dec_seq):
            qrow = q_ref[t:t + 1, :] * (A_HEAD_DIM ** -0.5)
            qbd_sc[t * A_HEADS:(t + 1) * A_HEADS, :] = qrow * hmask

    @pl.when(active & (ph == 0))
    def _scores():
        qbd = qbd_sc[...].astype(BF16)
        for r in range(PPS):
            s_sc[st * PPS + r] = _mm(qbd, k_refs[r][...].astype(BF16))

    @pl.when(active & (ph == 0) & (st == nsteps - 1))
    def _softmax():
        slope = slope_ref[...]
        tq = tq_ref[...]
        lane = lax.broadcasted_iota(jnp.int32, (NR, PAGE_SIZE), 1)
        lane_f = lane.astype(F32)
        q_of_row = lax.broadcasted_iota(jnp.int32, (NR, PAGE_SIZE), 0) // A_HEADS
        gate = jnp.full((NR, PAGE_SIZE), -jnp.inf, F32)
        for n in range(nb):
            tot = s_sc[ppb * n]
            for e in range(1, ppb):
                tot = tot + s_sc[ppb * n + e]
            gate = jnp.where(lane == n, jnp.sum(tot, axis=-1, keepdims=True) * (1.0 / MOBA_BLOCK), gate)
        sel = jnp.zeros((NR, PAGE_SIZE), F32)
        for _ in range(MOBA_TOPK):
            mx = jnp.max(gate, axis=-1, keepdims=True)
            first = jnp.min(jnp.where(gate == mx, lane_f, float(PAGE_SIZE)), axis=-1, keepdims=True)
            pick = lane_f == first
            sel = jnp.where(pick, 1.0, sel)
            gate = jnp.where(pick, -jnp.inf, gate)
        pad = jnp.zeros((PAGE_SIZE - kn_ref.shape[0], A_WIDTH), F32)
        qbd = qbd_sc[...].astype(BF16)
        s_own = _mm_nt(qbd, jnp.concatenate([kn_ref[...], pad], axis=0).astype(BF16))
        own_ok = (lane <= q_of_row) & (lane < dec_seq)
        s_own = jnp.where(own_ok, s_own - slope * (q_of_row - lane).astype(F32), -jnp.inf)
        m = jnp.max(s_own, axis=-1, keepdims=True)
        for n in range(nb):
            on = jnp.sum(jnp.where(lane == n, sel, 0.0), axis=-1, keepdims=True) > 0.0
            for e in range(ppb):
                pg = ppb * n + e
                sb = s_sc[pg] - slope * (tq - (lane_f + float(pg * PAGE_SIZE)))
                sb = jnp.where(on, sb, -jnp.inf)
                s_sc[pg] = sb
                m = jnp.maximum(m, jnp.max(sb, axis=-1, keepdims=True))
        p_own = jnp.exp(s_own - m)
        l = jnp.sum(p_own, axis=-1, keepdims=True)
        for pg in range(n_pages):
            pb = jnp.exp(s_sc[pg] - m)
            l = l + jnp.sum(pb, axis=-1, keepdims=True)
            p_sc[pg] = pb.astype(BF16)
        linv_sc[...] = jnp.broadcast_to(1.0 / l, (NR, LANES))
        v_own = jnp.concatenate([vn_ref[...], pad], axis=0).astype(BF16)
        acc_sc[...] = _mm(p_own.astype(BF16), v_own)

    @pl.when(active & (ph == 1))
    def _values():
        acc = acc_sc[...]
        for r in range(PPS):
            acc = acc + _mm_nt(p_sc[st * PPS + r], v_refs[r][...].astype(BF16))
        acc_sc[...] = acc

    @pl.when(active & (ph == 1) & (st == nsteps - 1))
    def _finish():
        linv = jnp.concatenate([linv_sc[...]] * (A_WIDTH // LANES), axis=1)
        a = acc_sc[...] * linv
        o_ref[...] = jnp.zeros_like(o_ref)
        for t in range(dec_seq):
            o_ref[t:t + 1, :] = jnp.sum(a[t * A_HEADS:(t + 1) * A_HEADS, :] * hmask, axis=0, keepdims=True)


N_PROMPT_SCRATCH = 7
N_PROMPT_IN = 4
N_SAMPLE_IN = 5


def _sample_item(step, n_items, nsteps):
    w = jnp.minimum(step, n_items - 1)
    return step < n_items, w // (2 * nsteps), (w // nsteps) % 2, w % nsteps


def _moba_kernel(pt_ref, *refs, nblk, n_pages, dec_seq, n_items):
    del pt_ref
    PPS = PAGES_PER_STEP
    prompt_in = refs[:N_PROMPT_IN]
    sample_in = refs[N_PROMPT_IN:N_PROMPT_IN + N_SAMPLE_IN]
    pages = refs[N_PROMPT_IN + N_SAMPLE_IN:N_PROMPT_IN + N_SAMPLE_IN + 2 * PPS]
    o_prompt, o_sample = refs[N_PROMPT_IN + N_SAMPLE_IN + 2 * PPS:N_PROMPT_IN + N_SAMPLE_IN + 2 * PPS + 2]
    scratch = refs[N_PROMPT_IN + N_SAMPLE_IN + 2 * PPS + 2:]
    b, pair, i = pl.program_id(0), pl.program_id(1), pl.program_id(2)
    _moba_prompt_body(pair, i, *prompt_in, o_prompt, *scratch[:N_PROMPT_SCRATCH], nblk=nblk)
    step = (b * pl.num_programs(1) + pair) * nblk + i
    active, _, ph, st = _sample_item(step, n_items, n_pages // PPS)
    _moba_sample_body(active, ph, st, *sample_in, pages[:PPS], pages[PPS:], o_sample,
                      *scratch[N_PROMPT_SCRATCH:], n_pages=n_pages, dec_seq=dec_seq)


def _moba(uf, kk, vv, usf, ks, vs, slopes, slope_rows, tq_rows, ck_t, cv_t, page_table, layer, Bp, L, Bs, tpad,
          dec_seq):
    nblk = L // MOBA_BLOCK
    PW = 2 * A_HEAD_DIM
    cpb = COL // PW
    npair = A_HEADS // 2
    n_pages = page_table.shape[1]
    PPS = PAGES_PER_STEP
    nsteps = n_pages // PPS
    n_items = Bs * 2 * nsteps
    assert n_items <= Bp * npair * nblk, "more sample page batches than prompt grid steps"
    NR = dec_seq * A_HEADS
    sample3 = lambda a: a.reshape(Bs, tpad, a.shape[-1])
    us3 = sample3(usf)

    def item(b, p, i):
        return _sample_item((b * npair + p) * nblk + i, n_items, nsteps)

    def seq_map(col):
        return lambda b, p, i, pt: (item(b, p, i)[1], 0, col)

    def kmap(r):
        def index(b, p, i, pt):
            _, seq, ph, st = item(b, p, i)
            return layer, pt[seq, (st * (1 - ph) + (nsteps - 1) * ph) * PPS + r], 0, 0
        return index

    def vmap_(r):
        def index(b, p, i, pt):
            _, seq, ph, st = item(b, p, i)
            return layer, pt[seq, (st * ph) * PPS + r], 0, 0
        return index

    page_blk = (None, None, A_WIDTH, PAGE_SIZE)
    in_specs = [pl.BlockSpec(memory_space=pltpu.SMEM),
                pl.BlockSpec((L, PW), lambda b, p, i, pt: (b, FB_QA * cpb + p)),
                pl.BlockSpec((L, PW), lambda b, p, i, pt: (b, p)),
                pl.BlockSpec((L, PW), lambda b, p, i, pt: (b, p)),
                pl.BlockSpec((NR, LANES), lambda b, p, i, pt: (0, 0)),
                pl.BlockSpec((NR, LANES), lambda b, p, i, pt: (0, 0)),
                pl.BlockSpec((None, tpad, COL), seq_map(FB_QA)),
                pl.BlockSpec((None, tpad, A_WIDTH), seq_map(0)),
                pl.BlockSpec((None, tpad, A_WIDTH), seq_map(0))]
    in_specs += [pl.BlockSpec(page_blk, kmap(r)) for r in range(PPS)]
    in_specs += [pl.BlockSpec(page_blk, vmap_(r)) for r in range(PPS)]
    grid_spec = pltpu.PrefetchScalarGridSpec(
        num_scalar_prefetch=1,
        grid=(Bp, npair, nblk),
        in_specs=in_specs,
        out_specs=[pl.BlockSpec((MOBA_BLOCK, PW), lambda b, p, i, pt: (b * nblk + i, p)),
                   pl.BlockSpec((None, tpad, A_WIDTH), seq_map(0))],
        scratch_shapes=[pltpu.VMEM((nblk, MOBA_BLOCK, 2 * PW), BF16),
                        pltpu.VMEM((nblk, 2, A_HEAD_DIM + BF16_ROWS, MOBA_BLOCK), BF16),
                        pltpu.VMEM((2 * nblk, PW), F32),
                        pltpu.VMEM((MOBA_BLOCK, MOBA_BLOCK), F32),
                        pltpu.VMEM((2, nblk, nblk, MOBA_BLOCK), F32),
                        pltpu.VMEM((2, nblk, MOBA_BLOCK, MOBA_BLOCK), F32),
                        pltpu.VMEM((nblk, PW, MOBA_BLOCK), F32),
                        pltpu.VMEM((NR, A_WIDTH), F32),
                        pltpu.VMEM((n_pages, NR, PAGE_SIZE), F32),
                        pltpu.VMEM((n_pages, NR, PAGE_SIZE), BF16),
                        pltpu.VMEM((NR, A_WIDTH), F32),
                        pltpu.VMEM((NR, LANES), F32)])
    oa, oas = pl.pallas_call(
        functools.partial(_moba_kernel, nblk=nblk, n_pages=n_pages, dec_seq=dec_seq, n_items=n_items),
        grid_spec=grid_spec,
        out_shape=[jax.ShapeDtypeStruct((Bp * L, A_WIDTH), F32), jax.ShapeDtypeStruct((Bs, tpad, A_WIDTH), F32)],
        compiler_params=_cparams("arbitrary", "arbitrary", "arbitrary"),
        name="moba",
    )(page_table, slopes, uf, kk, vv, slope_rows, tq_rows, us3, sample3(ks), sample3(vs),
      *([ck_t] * PPS), *([cv_t] * PPS))
    return oa, oas.reshape(Bs * tpad, A_WIDTH)


def _lru_kernel(xb_ref, zb_ref, conv0_ref, h0_ref, cw_ref, cb_ref, wa_ref, ba_ref, wx_ref, bx_ref, lam_ref,
                y_ref, conv1_ref, h1_ref, xp_sc, h_sc, a_sc, b_sc, *, Bn, tc, l_valid):
    c = pl.program_id(0)

    @pl.when(c == 0)
    def _():
        xp_sc[:, TAIL0:SUBLANES, :] = conv0_ref[...]
        h_sc[...] = h0_ref[...]

    xp_sc[:, SUBLANES:SUBLANES + tc, :] = xb_ref[...].astype(F32)
    xc = xp_sc[:, TAIL0:TAIL0 + tc, :] * cw_ref[0:1, :]
    for j in range(1, CONV_WIDTH):
        xc = xc + xp_sc[:, TAIL0 + j:TAIL0 + j + tc, :] * cw_ref[j:j + 1, :]
    xc = xc + cb_ref[...]
    nv = min(tc, l_valid)
    tail = xp_sc[:, TAIL0 + nv:SUBLANES + nv, :]
    conv1_ref[...] = tail
    xp_sc[:, TAIL0:SUBLANES, :] = tail

    x2 = xc.reshape(Bn * tc, B_WIDTH)
    x16 = x2.astype(BF16)
    r = _sigmoid(jnp.dot(x16, wa_ref[...], preferred_element_type=F32) + ba_ref[...])
    ig = _sigmoid(jnp.dot(x16, wx_ref[...], preferred_element_type=F32) + bx_ref[...])
    log_a = -LRU_C * r * _softplus(-lam_ref[...])
    a = jnp.exp(log_a)
    bin_ = jnp.sqrt(-jnp.tanh(log_a) * (a * a + 1.0)) * (ig * x2)
    a_sc[...] = a.reshape(Bn, tc, B_WIDTH)
    b_sc[...] = bin_.reshape(Bn, tc, B_WIDTH)

    def step(t, h):
        h = a_sc[:, pl.ds(t, 1), :] * h + b_sc[:, pl.ds(t, 1), :]
        a_sc[:, pl.ds(t, 1), :] = h
        return h

    h = lax.fori_loop(0, nv, step, h_sc[...])
    h_sc[...] = h
    h1_ref[...] = h
    y_ref[...] = _silu(zb_ref[...].astype(F32)) * a_sc[...]


def _lru(u3, conv0, h0, cw, cb, wa_bd, ba, wx_bd, bx, lam, tc, l_valid):
    Bn, Lp, _ = u3.shape
    row = lambda v: v.reshape(1, B_WIDTH)
    const2 = lambda c: (0, 0)
    return pl.pallas_call(
        functools.partial(_lru_kernel, Bn=Bn, tc=tc, l_valid=l_valid),
        grid=(Lp // tc,),
        in_specs=[pl.BlockSpec((Bn, tc, COL), lambda c: (0, c, HB_XB)),
                  pl.BlockSpec((Bn, tc, COL), lambda c: (0, c, HB_ZB)),
                  pl.BlockSpec((Bn, CONV_WIDTH - 1, B_WIDTH), lambda c: (0, 0, 0)),
                  pl.BlockSpec((Bn, 1, B_WIDTH), lambda c: (0, 0, 0)),
                  pl.BlockSpec((CONV_WIDTH, B_WIDTH), const2),
                  pl.BlockSpec((1, B_WIDTH), const2),
                  pl.BlockSpec((B_WIDTH, B_WIDTH), const2),
                  pl.BlockSpec((1, B_WIDTH), const2),
                  pl.BlockSpec((B_WIDTH, B_WIDTH), const2),
                  pl.BlockSpec((1, B_WIDTH), const2),
                  pl.BlockSpec((1, B_WIDTH), const2)],
        out_specs=[pl.BlockSpec((Bn, tc, B_WIDTH), lambda c: (0, c, 0)),
                   pl.BlockSpec((Bn, CONV_WIDTH - 1, B_WIDTH), lambda c: (0, 0, 0)),
                   pl.BlockSpec((Bn, 1, B_WIDTH), lambda c: (0, 0, 0))],
        out_shape=[jax.ShapeDtypeStruct((Bn, Lp, B_WIDTH), F32),
                   jax.ShapeDtypeStruct((Bn, CONV_WIDTH - 1, B_WIDTH), F32),
                   jax.ShapeDtypeStruct((Bn, 1, B_WIDTH), F32)],
        scratch_shapes=[pltpu.VMEM((Bn, SUBLANES + tc, B_WIDTH), F32),
                        pltpu.VMEM((Bn, 1, B_WIDTH), F32),
                        pltpu.VMEM((Bn, tc, B_WIDTH), F32),
                        pltpu.VMEM((Bn, tc, B_WIDTH), F32)],
        compiler_params=_cparams("arbitrary"),
        name="rglru",
    )(u3, u3, conv0, h0.reshape(Bn, 1, B_WIDTH), cw, row(cb), wa_bd, row(ba), wx_bd, row(bx), row(lam))


def _gdn_kernel(qkv_ref, beta_ref, alpha_ref, z_ref, conv0_ref, s0_ref, cw_ref, alog_ref, dtb_ref, nw_ref,
                o_ref, conv1_ref, s1_ref, xp_sc, s_sc, y_sc, *, tb, l_valid):
    C = DELTA_CHUNK
    R = max(tb, C)
    nc = R // C
    t = pl.program_id(1)

    @pl.when(t == 0)
    def _():
        xp_sc[TAIL0:SUBLANES, :] = conv0_ref[...]
        s_sc[...] = s0_ref[...]

    xp_sc[SUBLANES:SUBLANES + tb, :] = qkv_ref[...].astype(F32)
    y = xp_sc[TAIL0:TAIL0 + tb, :] * cw_ref[0:1, :]
    for j in range(1, CONV_WIDTH):
        y = y + xp_sc[TAIL0 + j:TAIL0 + j + tb, :] * cw_ref[j:j + 1, :]
    nv = min(tb, l_valid)
    tail = xp_sc[TAIL0 + nv:SUBLANES + nv, :]
    conv1_ref[...] = tail
    xp_sc[TAIL0:SUBLANES, :] = tail
    y = _silu(y)

    if nv < R:
        live = lax.broadcasted_iota(jnp.int32, (tb, 1), 0) < nv
        y_sc[...] = jnp.zeros_like(y_sc)
        y_sc[0:tb, :] = jnp.where(live, y, 0.0)
    else:
        live = None
        y_sc[...] = y

    ri = lax.broadcasted_iota(jnp.int32, (R, R), 0)
    ci = lax.broadcasted_iota(jnp.int32, (R, R), 1)
    same = (ri // C) == (ci // C)
    causal = same & (ri >= ci)
    strict = same & (ri > ci)
    diag = ri == ci
    eye = jnp.where(diag, 1.0, 0.0)
    ltri16 = jnp.where(causal, 1.0, 0.0).astype(BF16)

    def padrows(v):
        if tb == R:
            return v
        return jnp.concatenate([v, jnp.zeros((R - tb, v.shape[1]), F32)], axis=0)

    def widen(v):
        return v[:, 0:R] if R <= LANES else jnp.concatenate([v] * (R // LANES), axis=1)

    heads = range(C_HEADS)
    hsl = [slice(h * C_KEY_DIM, (h + 1) * C_KEY_DIM) for h in heads]

    def prepare(h):
        q = y_sc[:, h * C_KEY_DIM:(h + 1) * C_KEY_DIM]
        k = y_sc[:, C_QK_WIDTH + h * C_KEY_DIM:C_QK_WIDTH + (h + 1) * C_KEY_DIM]
        v = y_sc[:, 2 * C_QK_WIDTH + h * C_VAL_DIM:2 * C_QK_WIDTH + (h + 1) * C_VAL_DIM]
        q = q * lax.rsqrt(jnp.sum(q * q, axis=-1, keepdims=True) + EPS) * (C_KEY_DIM ** -0.5)
        k = k * lax.rsqrt(jnp.sum(k * k, axis=-1, keepdims=True) + EPS)
        beta = _sigmoid(beta_ref[:, hsl[h]])
        g = -jnp.exp(alog_ref[:, hsl[h]]) * _softplus(alpha_ref[:, hsl[h]] + dtb_ref[:, hsl[h]])
        if live is not None:
            beta = jnp.where(live, beta, 0.0)
            g = jnp.where(live, g, 0.0)
        beta = padrows(beta)
        g = padrows(g)
        g_hi, g_lo = _split2(g)
        g_lo2 = (g - g_hi.astype(F32) - g_lo.astype(F32)).astype(BF16)
        G = _mm(ltri16, g_hi) + (_mm(ltri16, g_lo) + _mm(ltri16, g_lo2))
        Gcol = widen(G)
        Grow = jnp.sum(jnp.where(diag, Gcol, 0.0), axis=0, keepdims=True)
        decay = jnp.exp(jnp.where(causal, Gcol - Grow, -jnp.inf))
        kb = k * beta
        k16 = k.astype(BF16)
        A = jnp.where(strict, _mm_nt(kb.astype(BF16), k16) * decay, 0.0)
        eG = jnp.exp(G)
        rhs16 = jnp.concatenate([v * beta, kb * eG], axis=1).astype(BF16)
        QK16 = (_mm_nt(q.astype(BF16), k16) * decay).astype(BF16)
        glast = jnp.concatenate([jnp.broadcast_to(G[(c + 1) * C - 1:(c + 1) * C, :], (C, LANES)) for c in range(nc)],
                                axis=0)
        M = -A
        return dict(M=M, T=eye + M, Ms=_split2(M), rhs16=rhs16, QK16=QK16, qg16=(q * eG).astype(BF16),
                    kg16=(k * jnp.exp(glast - G)).astype(BF16),
                    gl=[jnp.exp(G[(c + 1) * C - 1:(c + 1) * C, :]) for c in range(nc)])

    def square(s, it):
        s["M"] = _mm_split(s["Ms"], s["Ms"]) if it < NEUMANN_SPLIT_STEPS else _mm(s["Ms"][0], s["Ms"][0])

    def extend(s, it):
        if it < NEUMANN_SPLIT_STEPS:
            s["Ms"] = _split2(s["M"])
            s["T"] = s["T"] + _mm_split(_split2(s["T"]), s["Ms"])
        else:
            s["Ms"] = (s["M"].astype(BF16),)
            s["T"] = s["T"] + _mm(s["T"].astype(BF16), s["Ms"][0])

    def apply_inverse(h, s):
        UW = _mm(s["T"].astype(BF16), s["rhs16"])
        s["U"] = UW[:, 0:C_VAL_DIM]
        s["W16"] = UW[:, C_VAL_DIM:].astype(BF16)
        s["S"] = s_sc[h]
        s["vn"], s["oi"] = [], []

    n_sq = max(0, (min(nv, C) - 1).bit_length() - 1)
    st = [prepare(h) for h in heads]
    for it in range(n_sq):
        for s in st:
            square(s, it)
        for s in st:
            extend(s, it)
    for h in heads:
        apply_inverse(h, st[h])

    for c in range(nc):
        rs = slice(c * C, (c + 1) * C)
        for s in st:
            S16 = s["S"].astype(BF16)
            vn = s["U"][rs] - _mm(s["W16"][rs], S16)
            s["oi"].append(_mm(s["qg16"][rs], S16))
            s["S"] = s["S"] * s["gl"][c] + lax.dot_general(s["kg16"][rs], vn.astype(BF16), TN_DIMS,
                                                           preferred_element_type=F32)
            s["vn"].append(vn)
    for h, s in zip(heads, st):
        s_sc[h] = s["S"]
        s1_ref[h] = s["S"]
        o = jnp.concatenate(s["oi"], axis=0) + _mm(s["QK16"], jnp.concatenate(s["vn"], axis=0).astype(BF16))
        o = o * lax.rsqrt(jnp.mean(o * o, axis=-1, keepdims=True) + EPS) * nw_ref[...]
        o_ref[:, hsl[h]] = o[0:tb] * _silu(z_ref[:, hsl[h]].astype(F32))


def _gdn(uf3, uh3, conv0, s0, cw, alog_e, dtb_e, nw, tb, l_valid):
    Bn, Lp, _ = uf3.shape
    rows = max(tb, DELTA_CHUNK)
    const2 = lambda b, t: (0, 0)
    return pl.pallas_call(
        functools.partial(_gdn_kernel, tb=tb, l_valid=l_valid),
        grid=(Bn, Lp // tb),
        in_specs=[pl.BlockSpec((None, tb, C_CONV_DIM), lambda b, t: (b, t, HB_QKVC * COL // C_CONV_DIM)),
                  pl.BlockSpec((None, tb, COL), lambda b, t: (b, t, FB_BETA)),
                  pl.BlockSpec((None, tb, COL), lambda b, t: (b, t, FB_ALPHA)),
                  pl.BlockSpec((None, tb, COL), lambda b, t: (b, t, HB_ZC)),
                  pl.BlockSpec((None, CONV_WIDTH - 1, C_CONV_DIM), lambda b, t: (b, 0, 0)),
                  pl.BlockSpec((None, C_HEADS, C_KEY_DIM, C_VAL_DIM), lambda b, t: (b, 0, 0, 0)),
                  pl.BlockSpec((CONV_WIDTH, C_CONV_DIM), const2),
                  pl.BlockSpec((1, C_V_WIDTH), const2),
                  pl.BlockSpec((1, C_V_WIDTH), const2),
                  pl.BlockSpec((1, C_VAL_DIM), const2)],
        out_specs=[pl.BlockSpec((None, tb, C_V_WIDTH), lambda b, t: (b, t, 0)),
                   pl.BlockSpec((None, CONV_WIDTH - 1, C_CONV_DIM), lambda b, t: (b, 0, 0)),
                   pl.BlockSpec((None, C_HEADS, C_KEY_DIM, C_VAL_DIM), lambda b, t: (b, 0, 0, 0))],
        out_shape=[jax.ShapeDtypeStruct((Bn, Lp, C_V_WIDTH), F32),
                   jax.ShapeDtypeStruct((Bn, CONV_WIDTH - 1, C_CONV_DIM), F32),
                   jax.ShapeDtypeStruct((Bn, C_HEADS, C_KEY_DIM, C_VAL_DIM), F32)],
        scratch_shapes=[pltpu.VMEM((SUBLANES + tb, C_CONV_DIM), F32),
                        pltpu.VMEM((C_HEADS, C_KEY_DIM, C_VAL_DIM), F32),
                        pltpu.VMEM((rows, C_CONV_DIM), F32)],
        compiler_params=_cparams("parallel", "arbitrary"),
        name="gdn",
    )(uh3, uf3, uf3, uh3, conv0, s0, cw, alog_e, dtb_e, nw.reshape(1, C_VAL_DIM))


def _merge_kernel(h_ref, oa_ref, za_ref, ob_ref, oc_ref, g0_ref, g1_ref, g2_ref,
                  wpa_ref, wpb_ref, wpc_ref, wout_ref, fnw_ref, hn_ref, *, final_norm):
    def proj(x, w_ref):
        return jnp.dot(x.astype(BF16), w_ref[...], preferred_element_type=F32)

    f32 = lambda ref: ref[...].astype(F32)
    ya = proj(_silu(f32(za_ref)) * oa_ref[...], wpa_ref)
    yb = proj(ob_ref[...], wpb_ref)
    yc = proj(oc_ref[...], wpc_ref)
    merged = _sigmoid(f32(g0_ref)) * ya + _sigmoid(f32(g1_ref)) * yb + _sigmoid(f32(g2_ref)) * yc
    hn = h_ref[...] + proj(merged, wout_ref)
    if final_norm:
        hn = hn * lax.rsqrt(jnp.mean(hn * hn, axis=-1, keepdims=True) + EPS) * fnw_ref[...]
    hn_ref[...] = hn


def _merge(h2d, oa, uh, ob, oc, wpa, wpb, wpc, wout, fnw, tm, final_norm):
    T = h2d.shape[0]
    gpb = D_MODEL // COL
    tok = lambda w: pl.BlockSpec((tm, w), lambda i: (i, 0))
    wspec = lambda k: pl.BlockSpec((k, D_MODEL), lambda i: (0, 0))
    gate_spec = lambda g: pl.BlockSpec((tm, D_MODEL), lambda i: (i, HB_GATES // gpb + g))
    return pl.pallas_call(
        functools.partial(_merge_kernel, final_norm=final_norm),
        grid=(T // tm,),
        in_specs=[tok(D_MODEL), tok(A_WIDTH),
                  pl.BlockSpec((tm, COL), lambda i: (i, HB_ZA)),
                  tok(B_WIDTH), tok(C_V_WIDTH),
                  gate_spec(0), gate_spec(1), gate_spec(2),
                  wspec(A_WIDTH), wspec(B_WIDTH), wspec(C_V_WIDTH), wspec(D_MODEL),
                  pl.BlockSpec((1, D_MODEL), lambda i: (0, 0))],
        out_specs=tok(D_MODEL),
        out_shape=jax.ShapeDtypeStruct((T, D_MODEL), F32),
        compiler_params=_cparams("parallel"),
        name="merge",
    )(h2d, oa, uh, ob, oc, uh, uh, uh, wpa, wpb, wpc, wout, fnw.reshape(1, D_MODEL))


def _block_diag(w):
    g, n, _ = w.shape
    same = jnp.eye(g, dtype=bool)[:, None, :, None]
    return jnp.where(same, w[:, :, None, :], 0.0).reshape(g * n, g * n)


def _rearrange_w_in(w):
    sizes = (A_WIDTH, A_WIDTH, A_WIDTH, A_WIDTH, B_WIDTH, B_WIDTH, C_CONV_DIM, C_V_WIDTH, C_HEADS, C_HEADS,
             N_BRANCH * D_MODEL)
    offs = [0]
    for s in sizes:
        offs.append(offs[-1] + s)
    qa, ka, va, za, xb, zb, qkvc, zc, beta, alpha, gates = (w[:, offs[n]:offs[n + 1]] for n in range(len(sizes)))
    rep = lambda c: jnp.repeat(c, C_VAL_DIM, axis=1)
    return jnp.concatenate([qa, ka, va, rep(beta), rep(alpha), qkvc, za, gates, xb, zb, zc], axis=1).astype(BF16)


def kernel(x_prompt, x_sample, cache_k, cache_v, page_table, state_conv_b, state_h_b, state_conv_c, state_s_c,
           norm_w, w_in, conv_b_w, conv_b_b, lru_wa, lru_ba, lru_wx, lru_bx, lru_lambda,
           conv_c_w, dn_a_log, dn_dt_bias, dn_norm_w, w_pa, w_pb, w_pc, w_out, final_norm_w):
    Bp, L, _ = x_prompt.shape
    Bs, dec_seq, _ = x_sample.shape
    depth = w_in.shape[0]
    n_pool = cache_k.shape[1]
    n_pages = page_table.shape[1]
    past = n_pages * PAGE_SIZE
    tpad = BF16_ROWS
    assert dec_seq <= tpad and L % MOBA_BLOCK == 0 and past % MOBA_BLOCK == 0 and n_pages % PAGES_PER_STEP == 0
    assert MOBA_TOPK <= past // MOBA_BLOCK <= PAGE_SIZE and (dec_seq * A_HEADS) % BF16_ROWS == 0

    slopes = jnp.exp2(-8.0 * jnp.arange(1, A_HEADS + 1, dtype=F32) / A_HEADS)
    row = jnp.arange(dec_seq * A_HEADS)
    slope_rows = jnp.broadcast_to(slopes[row % A_HEADS][:, None], (row.size, LANES))
    tq_rows = jnp.broadcast_to((past + row // A_HEADS).astype(F32)[:, None], (row.size, LANES))
    ck_t = jnp.transpose(cache_k, (0, 1, 3, 4, 2)).reshape(depth, n_pool, A_WIDTH, PAGE_SIZE)
    cv_t = jnp.transpose(cache_v, (0, 1, 3, 4, 2)).reshape(depth, n_pool, A_WIDTH, PAGE_SIZE)

    hp = x_prompt.reshape(Bp * L, D_MODEL)
    hs = jnp.pad(x_sample, ((0, 0), (0, tpad - dec_seq), (0, 0))).reshape(Bs * tpad, D_MODEL)
    zeros = lambda *s: jnp.zeros(s, F32)
    tm_p = 2048 if (Bp * L) % 2048 == 0 else MOBA_BLOCK
    tm_merge = min(tm_p, 512)
    tm_s = Bs * tpad
    tc_p = 256 if L % 256 == 0 else L
    outs = [[] for _ in range(12)]
    for l in range(depth):
        w_re = _rearrange_w_in(w_in[l])
        wa_bd = _block_diag(lru_wa[l]).astype(BF16)
        wx_bd = _block_diag(lru_wx[l]).astype(BF16)
        alog_e = jnp.repeat(dn_a_log[l], C_VAL_DIM).reshape(1, C_V_WIDTH)
        dtb_e = jnp.repeat(dn_dt_bias[l], C_VAL_DIM).reshape(1, C_V_WIDTH)
        wpa, wpb, wpc, wout = (w.astype(BF16) for w in (w_pa[l], w_pb[l], w_pc[l], w_out[l]))
        last = l == depth - 1
        lru_w = (conv_b_w[l], conv_b_b[l], wa_bd, lru_ba[l], wx_bd, lru_bx[l], lru_lambda[l])
        gdn_w = (conv_c_w[l], alog_e, dtb_e, dn_norm_w[l])

        uf, kk, vv, uh = _inproj(hp, norm_w[l], w_re, tm_p)
        uf3 = uf.reshape(Bp, L, N_UF)
        uh3 = uh.reshape(Bp, L, N_UH)
        usf, kks, vvs, ush = _inproj(hs, norm_w[l], w_re, tm_s)
        oa, oas = _moba(uf, kk, vv, usf, kks, vvs, slopes, slope_rows, tq_rows, ck_t, cv_t, page_table, l,
                        Bp, L, Bs, tpad, dec_seq)
        ob, cbp, hbp = _lru(uh3, zeros(Bp, CONV_WIDTH - 1, B_WIDTH), zeros(Bp, B_WIDTH), *lru_w, tc_p, L)
        oc, ccp, scp = _gdn(uf3, uh3, zeros(Bp, CONV_WIDTH - 1, C_CONV_DIM),
                            zeros(Bp, C_HEADS, C_KEY_DIM, C_VAL_DIM), *gdn_w, tc_p, L)
        hp = _merge(hp, oa, uh, ob.reshape(Bp * L, B_WIDTH), oc.reshape(Bp * L, C_V_WIDTH),
                    wpa, wpb, wpc, wout, final_norm_w, tm_merge, last)
        kp = kk.reshape(Bp, L, A_HEADS, A_HEAD_DIM)
        vp = vv.reshape(Bp, L, A_HEADS, A_HEAD_DIM)

        usf3 = usf.reshape(Bs, tpad, N_UF)
        ush3 = ush.reshape(Bs, tpad, N_UH)
        obs, cbs, hbs = _lru(ush3, state_conv_b[l], state_h_b[l], *lru_w, tpad, dec_seq)
        ocs, ccs, scs = _gdn(usf3, ush3, state_conv_c[l], state_s_c[l], *gdn_w, tpad, dec_seq)
        hs = _merge(hs, oas, ush, obs.reshape(Bs * tpad, B_WIDTH), ocs.reshape(Bs * tpad, C_V_WIDTH),
                    wpa, wpb, wpc, wout, final_norm_w, tm_s, last)
        ks = kks.reshape(Bs, tpad, A_HEADS, A_HEAD_DIM)[:, :dec_seq]
        vs = vvs.reshape(Bs, tpad, A_HEADS, A_HEAD_DIM)[:, :dec_seq]

        hbp = hbp.reshape(Bp, B_WIDTH)
        hbs = hbs.reshape(Bs, B_WIDTH)
        for lst, val in zip(outs, (kp, vp, ks, vs, cbp, hbp, cbs, hbs, ccp, scp, ccs, scs)):
            lst.append(val)

    y_prompt = hp.reshape(Bp, L, D_MODEL)
    y_sample = hs.reshape(Bs, tpad, D_MODEL)[:, :dec_seq]
    return (y_prompt, y_sample) + tuple(jnp.stack(o) for o in outs)
```

```python
import functools

import jax
import jax.numpy as jnp
from jax import lax
from jax.experimental import pallas as pl
from jax.experimental.pallas import tpu as pltpu

F32 = jnp.float32
BF16 = jnp.bfloat16
HIGHEST = lax.Precision.HIGHEST

D_MODEL = 1024
A_HEADS = 8
A_HEAD_DIM = 64
A_WIDTH = A_HEADS * A_HEAD_DIM
MOBA_BLOCK = 256
MOBA_TOPK = 3
PAGE_SIZE = 128
B_WIDTH = 512
B_BLOCKS = 8
LRU_C = 8.0
CONV_WIDTH = 4
C_HEADS = 4
C_KEY_DIM = 128
C_VAL_DIM = 128
C_QK_WIDTH = C_HEADS * C_KEY_DIM
C_V_WIDTH = C_HEADS * C_VAL_DIM
C_CONV_DIM = 2 * C_QK_WIDTH + C_V_WIDTH
DELTA_CHUNK = 64
NEUMANN_SPLIT_STEPS = 2
N_BRANCH = 3
EPS = 1e-6

COL = 512
FB_QA, FB_KA, FB_VA, FB_BETA, FB_ALPHA = 0, 1, 2, 3, 4
N_FBLK = 5
HB_QKVC, HB_ZA, HB_GATES, HB_XB, HB_ZB, HB_ZC = 0, 3, 4, 10, 11, 12
N_HBLK = 13
N_UF = N_FBLK * COL
N_UH = N_HBLK * COL

SUBLANES = 8
LANES = 128
BF16_ROWS = 16
TAIL0 = SUBLANES - (CONV_WIDTH - 1)
VMEM_LIMIT = 56 * 1024 * 1024

NT_DIMS = (((1,), (1,)), ((), ()))
TN_DIMS = (((0,), (0,)), ((), ()))


def _cparams(*sem):
    return pltpu.CompilerParams(dimension_semantics=sem, vmem_limit_bytes=VMEM_LIMIT)


def _sigmoid(x):
    return 0.5 * jnp.tanh(0.5 * x) + 0.5


def _silu(x):
    return x * _sigmoid(x)


def _softplus(x):
    return jnp.maximum(x, 0.0) + jnp.log1p(jnp.exp(-jnp.abs(x)))


def _mm(a, b):
    return jnp.dot(a, b, preferred_element_type=F32)


def _mm_nt(a, b):
    return lax.dot_general(a, b, NT_DIMS, preferred_element_type=F32)


def _split2(x):
    hi = x.astype(BF16)
    lo = (x - hi.astype(F32)).astype(BF16)
    return hi, lo


def _mm_split(a, b):
    (ah, al), (bh, bl) = a, b
    return _mm(ah, bh) + (_mm(ah, bl) + _mm(al, bh))


def _inproj_kernel(h_ref, nw_ref, w_ref, uf_ref, uh_ref, xn_sc):
    j = pl.program_id(1)

    @pl.when(j == 0)
    def _():
        x = h_ref[...]
        y = x * lax.rsqrt(jnp.mean(x * x, axis=-1, keepdims=True) + EPS) * nw_ref[...]
        xn_sc[...] = y.astype(BF16)

    def block():
        return jnp.dot(xn_sc[...], w_ref[...], preferred_element_type=F32)

    @pl.when(j < N_FBLK)
    def _():
        uf_ref[...] = block()

    @pl.when(j >= N_FBLK)
    def _():
        uh_ref[...] = block().astype(BF16)


def _inproj(h2d, norm_w, w_re, tm):
    T = h2d.shape[0]
    return pl.pallas_call(
        _inproj_kernel,
        grid=(T // tm, N_FBLK + N_HBLK),
        in_specs=[pl.BlockSpec((tm, D_MODEL), lambda i, j: (i, 0)),
                  pl.BlockSpec((1, D_MODEL), lambda i, j: (0, 0)),
                  pl.BlockSpec((D_MODEL, COL), lambda i, j: (0, j))],
        out_specs=[pl.BlockSpec((tm, COL), lambda i, j: (i, jnp.minimum(j, N_FBLK - 1))),
                   pl.BlockSpec((tm, COL), lambda i, j: (i, jnp.maximum(j - N_FBLK, 0)))],
        out_shape=[jax.ShapeDtypeStruct((T, N_UF), F32), jax.ShapeDtypeStruct((T, N_UH), BF16)],
        scratch_shapes=[pltpu.VMEM((tm, D_MODEL), BF16)],
        compiler_params=_cparams("parallel", "arbitrary"),
        name="inproj",
    )(h2d, norm_w.reshape(1, D_MODEL), w_re)


TILES_WIDE = 4
TILES_NARROW = 2

def _topk_select(gate, n_past, n_iota):
    nb = gate.shape[0]
    g = jnp.where(n_iota < n_past, gate, -jnp.inf)
    rank = jnp.zeros_like(g)
    for m in range(min(nb, n_past) if isinstance(n_past, int) else nb):
        gm = g[m:m + 1, :]
        rank = rank + jnp.where(n_iota > m, jnp.where(gm >= g, 1.0, 0.0), jnp.where(gm > g, 1.0, 0.0))
    return jnp.where(n_iota < n_past, jnp.where(rank < MOBA_TOPK, 1.0, 0.0), 0.0)


def _moba_prompt_body(pair, i, slopes_ref, q_ref, k_ref, v_ref, o_ref, kb_sc, vt_sc, means_sc, cmask_sc, sel_sc, s_sc,
                      qt_sc, *, nblk):
    BLK, HD, PW = MOBA_BLOCK, A_HEAD_DIM, 2 * A_HEAD_DIM
    lane = lax.broadcasted_iota(jnp.int32, (1, PW), 1)

    @pl.when(i == 0)
    def _init():
        c_col = lax.broadcasted_iota(jnp.int32, (BLK, PW), 0).astype(F32)
        lane_a = lax.broadcasted_iota(jnp.int32, (BLK, PW), 1)
        key_off = jnp.where(lane_a == 0, c_col, 0.0).astype(BF16)
        ones = jnp.ones((BF16_ROWS, BLK), BF16)
        for n in range(nblk):
            kblk = k_ref[n * BLK:(n + 1) * BLK, :]
            mean = jnp.sum(kblk, axis=0, keepdims=True) * (1.0 / BLK)
            means_sc[n:n + 1, :] = jnp.where(lane < HD, mean, 0.0)
            means_sc[nblk + n:nblk + n + 1, :] = jnp.where(lane < HD, 0.0, mean)
            kb_sc[n, :, 0:PW] = kblk.astype(BF16)
            kb_sc[n, :, PW:2 * PW] = key_off
            vt = v_ref[n * BLK:(n + 1) * BLK, :].T
            for hh in range(2):
                vt_sc[n, hh, 0:HD, :] = vt[hh * HD:(hh + 1) * HD, :].astype(BF16)
                vt_sc[n, hh, HD:HD + BF16_ROWS, :] = ones
        c_io = lax.broadcasted_iota(jnp.int32, (BLK, BLK), 0)
        r_io = lax.broadcasted_iota(jnp.int32, (BLK, BLK), 1)
        cmask_sc[...] = jnp.where(c_io <= r_io, 0.0, -jnp.inf)
        n_iota = lax.broadcasted_iota(jnp.int32, (nblk, BLK), 0)
        for ib in range(nblk):
            qt_b = (q_ref[ib * BLK:(ib + 1) * BLK, :] * (HD ** -0.5)).T
            qt_sc[ib] = qt_b
            gate2 = jnp.dot(means_sc[...], qt_b, precision=HIGHEST, preferred_element_type=F32)
            for hh in range(2):
                sel_sc[hh, ib] = _topk_select(gate2[hh * nblk:(hh + 1) * nblk, :], ib, n_iota)

    qt = qt_sc[i]
    dim_row = lax.broadcasted_iota(jnp.int32, (PW, BLK), 0)
    qaugs, slopes = [], []
    for hh in range(2):
        slope = slopes_ref[pair * 2 + hh]
        qh = jnp.where((dim_row // HD) == hh, qt, 0.0).astype(BF16)
        slope_row = jnp.where(dim_row == 0, slope, 0.0).astype(BF16)
        qaugs.append(jnp.concatenate([qh, slope_row], axis=0))
        slopes.append(slope)

    def colmax8(s):
        out = s[0:SUBLANES]
        for r in range(1, BLK // SUBLANES):
            out = jnp.maximum(out, s[r * SUBLANES:(r + 1) * SUBLANES])
        return out

    def past_tile(t, hh):
        j = jnp.minimum(t, i - 1)
        on = sel_sc[hh, i, pl.ds(j, 1), :] * jnp.where(t < i, 1.0, 0.0) > 0.0
        delta = -slopes[hh] * ((i - j) * BLK).astype(F32)
        return j, on, delta

    s_own = [_mm(kb_sc[i], qaugs[hh]) + cmask_sc[...] for hh in range(2)]
    n_wide = i // TILES_WIDE
    t_narrow = n_wide * TILES_WIDE
    n_narrow = (i - t_narrow + TILES_NARROW - 1) // TILES_NARROW

    def max_trip(width, t0):
        def trip(tt, mx):
            mx = list(mx)
            for e in range(width):
                for hh in range(2):
                    j, on, delta = past_tile(t0 + width * tt + e, hh)
                    s = _mm(kb_sc[j], qaugs[hh])
                    s_sc[hh, j] = s
                    mx[hh] = jnp.maximum(mx[hh], jnp.where(on, colmax8(s) + delta, -jnp.inf))
            return tuple(mx)
        return trip

    mx = lax.fori_loop(0, n_wide, max_trip(TILES_WIDE, 0), tuple(colmax8(s) for s in s_own))
    mx = lax.fori_loop(0, n_narrow, max_trip(TILES_NARROW, t_narrow), mx)
    m = [jnp.max(x, axis=0, keepdims=True) for x in mx]

    def pv_trip(width, t0):
        def trip(tt, acc):
            acc = list(acc)
            for hh in range(2):
                ps, vts = [], []
                for e in range(width):
                    j, on, delta = past_tile(t0 + width * tt + e, hh)
                    p = jnp.exp(s_sc[hh, j] - jnp.where(on, m[hh] - delta, jnp.inf))
                    ps.append(p.astype(BF16))
                    vts.append(vt_sc[j, hh])
                acc[hh] = acc[hh] + _mm(jnp.concatenate(vts, axis=1), jnp.concatenate(ps, axis=0))
            return tuple(acc)
        return trip

    acc0 = tuple(_mm(vt_sc[i, hh], jnp.exp(s_own[hh] - m[hh]).astype(BF16)) for hh in range(2))
    acc = lax.fori_loop(0, n_wide, pv_trip(TILES_WIDE, 0), acc0)
    acc = lax.fori_loop(0, n_narrow, pv_trip(TILES_NARROW, t_narrow), acc)
    outs = [a[0:HD, :] / a[HD:HD + 1, :] for a in acc]
    o_ref[...] = jnp.concatenate(outs, axis=0).T


PAGES_PER_STEP = 16


def _moba_sample_body(active, ph, st, slope_ref, tq_ref, q_ref, kn_ref, vn_ref, k_refs, v_refs, o_ref,
                      qbd_sc, s_sc, p_sc, acc_sc, linv_sc, *, n_pages, dec_seq):
    PPS = PAGES_PER_STEP
    nsteps = n_pages // PPS
    ppb = MOBA_BLOCK // PAGE_SIZE
    nb = n_pages // ppb
    NR = dec_seq * A_HEADS
    head_of_row = lax.broadcasted_iota(jnp.int32, (A_HEADS, A_WIDTH), 0)
    head_of_lane = lax.broadcasted_iota(jnp.int32, (A_HEADS, A_WIDTH), 1) // A_HEAD_DIM
    hmask = jnp.where(head_of_row == head_of_lane, 1.0, 0.0)

    @pl.when(active & (ph == 0) & (st == 0))
    def _init():
        for t in range(dec_seq):
            qrow = q_ref[t:t + 1, :] * (A_HEAD_DIM ** -0.5)
            qbd_sc[t * A_HEADS:(t + 1) * A_HEADS, :] = qrow * hmask

    @pl.when(active & (ph == 0))
    def _scores():
        qbd = qbd_sc[...].astype(BF16)
        for r in range(PPS):
            s_sc[st * PPS + r] = _mm(qbd, k_refs[r][...].astype(BF16))

    @pl.when(active & (ph == 0) & (st == nsteps - 1))
    def _softmax():
        slope = slope_ref[...]
        tq = tq_ref[...]
        lane = lax.broadcasted_iota(jnp.int32, (NR, PAGE_SIZE), 1)
        lane_f = lane.astype(F32)
        q_of_row = lax.broadcasted_iota(jnp.int32, (NR, PAGE_SIZE), 0) // A_HEADS
        gate = jnp.full((NR, PAGE_SIZE), -jnp.inf, F32)
        for n in range(nb):
            tot = s_sc[ppb * n]
            for e in range(1, ppb):
                tot = tot + s_sc[ppb * n + e]
            gate = jnp.where(lane == n, jnp.sum(tot, axis=-1, keepdims=True) * (1.0 / MOBA_BLOCK), gate)
        sel = jnp.zeros((NR, PAGE_SIZE), F32)
        for _ in range(MOBA_TOPK):
            mx = jnp.max(gate, axis=-1, keepdims=True)
            first = jnp.min(jnp.where(gate == mx, lane_f, float(PAGE_SIZE)), axis=-1, keepdims=True)
            pick = lane_f == first
            sel = jnp.where(pick, 1.0, sel)
            gate = jnp.where(pick, -jnp.inf, gate)
        pad = jnp.zeros((PAGE_SIZE - kn_ref.shape[0], A_WIDTH), F32)
        qbd = qbd_sc[...].astype(BF16)
        s_own = _mm_nt(qbd, jnp.concatenate([kn_ref[...], pad], axis=0).astype(BF16))
        own_ok = (lane <= q_of_row) & (lane < dec_seq)
        s_own = jnp.where(own_ok, s_own - slope * (q_of_row - lane).astype(F32), -jnp.inf)
        m = jnp.max(s_own, axis=-1, keepdims=True)
        for n in range(nb):
            on = jnp.sum(jnp.where(lane == n, sel, 0.0), axis=-1, keepdims=True) > 0.0
            for e in range(ppb):
                pg = ppb * n + e
                sb = s_sc[pg] - slope * (tq - (lane_f + float(pg * PAGE_SIZE)))
                sb = jnp.where(on, sb, -jnp.inf)
                s_sc[pg] = sb
                m = jnp.maximum(m, jnp.max(sb, axis=-1, keepdims=True))
        p_own = jnp.exp(s_own - m)
        l = jnp.sum(p_own, axis=-1, keepdims=True)
        for pg in range(n_pages):
            pb = jnp.exp(s_sc[pg] - m)
            l = l + jnp.sum(pb, axis=-1, keepdims=True)
            p_sc[pg] = pb.astype(BF16)
        linv_sc[...] = jnp.broadcast_to(1.0 / l, (NR, LANES))
        v_own = jnp.concatenate([vn_ref[...], pad], axis=0).astype(BF16)
        acc_sc[...] = _mm(p_own.astype(BF16), v_own)

    @pl.when(active & (ph == 1))
    def _values():
        acc = acc_sc[...]
        for r in range(PPS):
            acc = acc + _mm_nt(p_sc[st * PPS + r], v_refs[r][...].astype(BF16))
        acc_sc[...] = acc

    @pl.when(active & (ph == 1) & (st == nsteps - 1))
    def _finish():
        linv = jnp.concatenate([linv_sc[...]] * (A_WIDTH // LANES), axis=1)
        a = acc_sc[...] * linv
        o_ref[...] = jnp.zeros_like(o_ref)
        for t in range(dec_seq):
            o_ref[t:t + 1, :] = jnp.sum(a[t * A_HEADS:(t + 1) * A_HEADS, :] * hmask, axis=0, keepdims=True)


N_PROMPT_SCRATCH = 7
N_PROMPT_IN = 4
N_SAMPLE_IN = 5


def _sample_item(step, n_items, nsteps):
    w = jnp.minimum(step, n_items - 1)
    return step < n_items, w // (2 * nsteps), (w // nsteps) % 2, w % nsteps


def _moba_kernel(pt_ref, *refs, nblk, n_pages, dec_seq, n_items):
    del pt_ref
    PPS = PAGES_PER_STEP
    prompt_in = refs[:N_PROMPT_IN]
    sample_in = refs[N_PROMPT_IN:N_PROMPT_IN + N_SAMPLE_IN]
    pages = refs[N_PROMPT_IN + N_SAMPLE_IN:N_PROMPT_IN + N_SAMPLE_IN + 2 * PPS]
    o_prompt, o_sample = refs[N_PROMPT_IN + N_SAMPLE_IN + 2 * PPS:N_PROMPT_IN + N_SAMPLE_IN + 2 * PPS + 2]
    scratch = refs[N_PROMPT_IN + N_SAMPLE_IN + 2 * PPS + 2:]
    b, pair, i = pl.program_id(0), pl.program_id(1), pl.program_id(2)
    _moba_prompt_body(pair, i, *prompt_in, o_prompt, *scratch[:N_PROMPT_SCRATCH], nblk=nblk)
    step = (b * pl.num_programs(1) + pair) * nblk + i
    active, _, ph, st = _sample_item(step, n_items, n_pages // PPS)
    _moba_sample_body(active, ph, st, *sample_in, pages[:PPS], pages[PPS:], o_sample,
                      *scratch[N_PROMPT_SCRATCH:], n_pages=n_pages, dec_seq=dec_seq)


def _moba(uf, usf, slopes, slope_rows, tq_rows, ck_t, cv_t, page_table, layer, Bp, L, Bs, tpad, dec_seq):
    nblk = L // MOBA_BLOCK
    PW = 2 * A_HEAD_DIM
    cpb = COL // PW
    npair = A_HEADS // 2
    n_pages = page_table.shape[1]
    PPS = PAGES_PER_STEP
    nsteps = n_pages // PPS
    n_items = Bs * 2 * nsteps
    assert n_items <= Bp * npair * nblk, "more sample page batches than prompt grid steps"
    NR = dec_seq * A_HEADS
    us3 = usf.reshape(Bs, tpad, N_UF)

    def item(b, p, i):
        return _sample_item((b * npair + p) * nblk + i, n_items, nsteps)

    def seq_map(col):
        return lambda b, p, i, pt: (item(b, p, i)[1], 0, col)

    def kmap(r):
        def index(b, p, i, pt):
            _, seq, ph, st = item(b, p, i)
            return layer, pt[seq, (st * (1 - ph) + (nsteps - 1) * ph) * PPS + r], 0, 0
        return index

    def vmap_(r):
        def index(b, p, i, pt):
            _, seq, ph, st = item(b, p, i)
            return layer, pt[seq, (st * ph) * PPS + r], 0, 0
        return index

    page_blk = (None, None, A_WIDTH, PAGE_SIZE)
    in_specs = [pl.BlockSpec(memory_space=pltpu.SMEM),
                pl.BlockSpec((L, PW), lambda b, p, i, pt: (b, FB_QA * cpb + p)),
                pl.BlockSpec((L, PW), lambda b, p, i, pt: (b, FB_KA * cpb + p)),
                pl.BlockSpec((L, PW), lambda b, p, i, pt: (b, FB_VA * cpb + p)),
                pl.BlockSpec((NR, LANES), lambda b, p, i, pt: (0, 0)),
                pl.BlockSpec((NR, LANES), lambda b, p, i, pt: (0, 0)),
                pl.BlockSpec((None, tpad, COL), seq_map(FB_QA)),
                pl.BlockSpec((None, tpad, COL), seq_map(FB_KA)),
                pl.BlockSpec((None, tpad, COL), seq_map(FB_VA))]
    in_specs += [pl.BlockSpec(page_blk, kmap(r)) for r in range(PPS)]
    in_specs += [pl.BlockSpec(page_blk, vmap_(r)) for r in range(PPS)]
    grid_spec = pltpu.PrefetchScalarGridSpec(
        num_scalar_prefetch=1,
        grid=(Bp, npair, nblk),
        in_specs=in_specs,
        out_specs=[pl.BlockSpec((MOBA_BLOCK, PW), lambda b, p, i, pt: (b * nblk + i, p)),
                   pl.BlockSpec((None, tpad, A_WIDTH), seq_map(0))],
        scratch_shapes=[pltpu.VMEM((nblk, MOBA_BLOCK, 2 * PW), BF16),
                        pltpu.VMEM((nblk, 2, A_HEAD_DIM + BF16_ROWS, MOBA_BLOCK), BF16),
                        pltpu.VMEM((2 * nblk, PW), F32),
                        pltpu.VMEM((MOBA_BLOCK, MOBA_BLOCK), F32),
                        pltpu.VMEM((2, nblk, nblk, MOBA_BLOCK), F32),
                        pltpu.VMEM((2, nblk, MOBA_BLOCK, MOBA_BLOCK), F32),
                        pltpu.VMEM((nblk, PW, MOBA_BLOCK), F32),
                        pltpu.VMEM((NR, A_WIDTH), F32),
                        pltpu.VMEM((n_pages, NR, PAGE_SIZE), F32),
                        pltpu.VMEM((n_pages, NR, PAGE_SIZE), BF16),
                        pltpu.VMEM((NR, A_WIDTH), F32),
                        pltpu.VMEM((NR, LANES), F32)])
    oa, oas = pl.pallas_call(
        functools.partial(_moba_kernel, nblk=nblk, n_pages=n_pages, dec_seq=dec_seq, n_items=n_items),
        grid_spec=grid_spec,
        out_shape=[jax.ShapeDtypeStruct((Bp * L, A_WIDTH), F32), jax.ShapeDtypeStruct((Bs, tpad, A_WIDTH), F32)],
        compiler_params=_cparams("arbitrary", "arbitrary", "arbitrary"),
        name="moba",
    )(page_table, slopes, uf, uf, uf, slope_rows, tq_rows, us3, us3, us3, *([ck_t] * PPS), *([cv_t] * PPS))
    return oa, oas.reshape(Bs * tpad, A_WIDTH)


def _lru_kernel(xb_ref, zb_ref, conv0_ref, h0_ref, cw_ref, cb_ref, wa_ref, ba_ref, wx_ref, bx_ref, lam_ref,
                y_ref, conv1_ref, h1_ref, xp_sc, h_sc, a_sc, b_sc, *, Bn, tc, l_valid):
    c = pl.program_id(0)

    @pl.when(c == 0)
    def _():
        xp_sc[:, TAIL0:SUBLANES, :] = conv0_ref[...]
        h_sc[...] = h0_ref[...]

    xp_sc[:, SUBLANES:SUBLANES + tc, :] = xb_ref[...].astype(F32)
    xc = xp_sc[:, TAIL0:TAIL0 + tc, :] * cw_ref[0:1, :]
    for j in range(1, CONV_WIDTH):
        xc = xc + xp_sc[:, TAIL0 + j:TAIL0 + j + tc, :] * cw_ref[j:j + 1, :]
    xc = xc + cb_ref[...]
    nv = min(tc, l_valid)
    tail = xp_sc[:, TAIL0 + nv:SUBLANES + nv, :]
    conv1_ref[...] = tail
    xp_sc[:, TAIL0:SUBLANES, :] = tail

    x2 = xc.reshape(Bn * tc, B_WIDTH)
    x16 = x2.astype(BF16)
    r = _sigmoid(jnp.dot(x16, wa_ref[...], preferred_element_type=F32) + ba_ref[...])
    ig = _sigmoid(jnp.dot(x16, wx_ref[...], preferred_element_type=F32) + bx_ref[...])
    log_a = -LRU_C * r * _softplus(-lam_ref[...])
    a = jnp.exp(log_a)
    bin_ = jnp.sqrt(-jnp.tanh(log_a) * (a * a + 1.0)) * (ig * x2)
    a_sc[...] = a.reshape(Bn, tc, B_WIDTH)
    b_sc[...] = bin_.reshape(Bn, tc, B_WIDTH)

    def step(t, h):
        h = a_sc[:, pl.ds(t, 1), :] * h + b_sc[:, pl.ds(t, 1), :]
        a_sc[:, pl.ds(t, 1), :] = h
        return h

    h = lax.fori_loop(0, nv, step, h_sc[...])
    h_sc[...] = h
    h1_ref[...] = h
    y_ref[...] = _silu(zb_ref[...].astype(F32)) * a_sc[...]


def _lru(u3, conv0, h0, cw, cb, wa_bd, ba, wx_bd, bx, lam, tc, l_valid):
    Bn, Lp, _ = u3.shape
    row = lambda v: v.reshape(1, B_WIDTH)
    const2 = lambda c: (0, 0)
    return pl.pallas_call(
        functools.partial(_lru_kernel, Bn=Bn, tc=tc, l_valid=l_valid),
        grid=(Lp // tc,),
        in_specs=[pl.BlockSpec((Bn, tc, COL), lambda c: (0, c, HB_XB)),
                  pl.BlockSpec((Bn, tc, COL), lambda c: (0, c, HB_ZB)),
                  pl.BlockSpec((Bn, CONV_WIDTH - 1, B_WIDTH), lambda c: (0, 0, 0)),
                  pl.BlockSpec((Bn, 1, B_WIDTH), lambda c: (0, 0, 0)),
                  pl.BlockSpec((CONV_WIDTH, B_WIDTH), const2),
                  pl.BlockSpec((1, B_WIDTH), const2),
                  pl.BlockSpec((B_WIDTH, B_WIDTH), const2),
                  pl.BlockSpec((1, B_WIDTH), const2),
                  pl.BlockSpec((B_WIDTH, B_WIDTH), const2),
                  pl.BlockSpec((1, B_WIDTH), const2),
                  pl.BlockSpec((1, B_WIDTH), const2)],
        out_specs=[pl.BlockSpec((Bn, tc, B_WIDTH), lambda c: (0, c, 0)),
                   pl.BlockSpec((Bn, CONV_WIDTH - 1, B_WIDTH), lambda c: (0, 0, 0)),
                   pl.BlockSpec((Bn, 1, B_WIDTH), lambda c: (0, 0, 0))],
        out_shape=[jax.ShapeDtypeStruct((Bn, Lp, B_WIDTH), F32),
                   jax.ShapeDtypeStruct((Bn, CONV_WIDTH - 1, B_WIDTH), F32),
                   jax.ShapeDtypeStruct((Bn, 1, B_WIDTH), F32)],
        scratch_shapes=[pltpu.VMEM((Bn, SUBLANES + tc, B_WIDTH), F32),
                        pltpu.VMEM((Bn, 1, B_WIDTH), F32),
                        pltpu.VMEM((Bn, tc, B_WIDTH), F32),
                        pltpu.VMEM((Bn, tc, B_WIDTH), F32)],
        compiler_params=_cparams("arbitrary"),
        name="rglru",
    )(u3, u3, conv0, h0.reshape(Bn, 1, B_WIDTH), cw, row(cb), wa_bd, row(ba), wx_bd, row(bx), row(lam))


def _gdn_kernel(qkv_ref, beta_ref, alpha_ref, z_ref, conv0_ref, s0_ref, cw_ref, alog_ref, dtb_ref, nw_ref,
                o_ref, conv1_ref, s1_ref, xp_sc, s_sc, y_sc, *, tb, l_valid):
    C = DELTA_CHUNK
    R = max(tb, C)
    nc = R // C
    t = pl.program_id(1)

    @pl.when(t == 0)
    def _():
        xp_sc[TAIL0:SUBLANES, :] = conv0_ref[...]
        s_sc[...] = s0_ref[...]

    xp_sc[SUBLANES:SUBLANES + tb, :] = qkv_ref[...].astype(F32)
    y = xp_sc[TAIL0:TAIL0 + tb, :] * cw_ref[0:1, :]
    for j in range(1, CONV_WIDTH):
        y = y + xp_sc[TAIL0 + j:TAIL0 + j + tb, :] * cw_ref[j:j + 1, :]
    nv = min(tb, l_valid)
    tail = xp_sc[TAIL0 + nv:SUBLANES + nv, :]
    conv1_ref[...] = tail
    xp_sc[TAIL0:SUBLANES, :] = tail
    y = _silu(y)

    if nv < R:
        live = lax.broadcasted_iota(jnp.int32, (tb, 1), 0) < nv
        y_sc[...] = jnp.zeros_like(y_sc)
        y_sc[0:tb, :] = jnp.where(live, y, 0.0)
    else:
        live = None
        y_sc[...] = y

    ri = lax.broadcasted_iota(jnp.int32, (R, R), 0)
    ci = lax.broadcasted_iota(jnp.int32, (R, R), 1)
    same = (ri // C) == (ci // C)
    causal = same & (ri >= ci)
    strict = same & (ri > ci)
    diag = ri == ci
    eye = jnp.where(diag, 1.0, 0.0)
    ltri16 = jnp.where(causal, 1.0, 0.0).astype(BF16)

    def padrows(v):
        if tb == R:
            return v
        return jnp.concatenate([v, jnp.zeros((R - tb, v.shape[1]), F32)], axis=0)

    def widen(v):
        return v[:, 0:R] if R <= LANES else jnp.concatenate([v] * (R // LANES), axis=1)

    heads = range(C_HEADS)
    hsl = [slice(h * C_KEY_DIM, (h + 1) * C_KEY_DIM) for h in heads]

    def prepare(h):
        q = y_sc[:, h * C_KEY_DIM:(h + 1) * C_KEY_DIM]
        k = y_sc[:, C_QK_WIDTH + h * C_KEY_DIM:C_QK_WIDTH + (h + 1) * C_KEY_DIM]
        v = y_sc[:, 2 * C_QK_WIDTH + h * C_VAL_DIM:2 * C_QK_WIDTH + (h + 1) * C_VAL_DIM]
        q = q * lax.rsqrt(jnp.sum(q * q, axis=-1, keepdims=True) + EPS) * (C_KEY_DIM ** -0.5)
        k = k * lax.rsqrt(jnp.sum(k * k, axis=-1, keepdims=True) + EPS)
        beta = _sigmoid(beta_ref[:, hsl[h]])
        g = -jnp.exp(alog_ref[:, hsl[h]]) * _softplus(alpha_ref[:, hsl[h]] + dtb_ref[:, hsl[h]])
        if live is not None:
            beta = jnp.where(live, beta, 0.0)
            g = jnp.where(live, g, 0.0)
        beta = padrows(beta)
        g = padrows(g)
        g_hi, g_lo = _split2(g)
        g_lo2 = (g - g_hi.astype(F32) - g_lo.astype(F32)).astype(BF16)
        G = _mm(ltri16, g_hi) + (_mm(ltri16, g_lo) + _mm(ltri16, g_lo2))
        Gcol = widen(G)
        Grow = jnp.sum(jnp.where(diag, Gcol, 0.0), axis=0, keepdims=True)
        decay = jnp.exp(jnp.where(causal, Gcol - Grow, -jnp.inf))
        kb = k * beta
        k16 = k.astype(BF16)
        A = jnp.where(strict, _mm_nt(kb.astype(BF16), k16) * decay, 0.0)
        eG = jnp.exp(G)
        rhs16 = jnp.concatenate([v * beta, kb * eG], axis=1).astype(BF16)
        QK16 = (_mm_nt(q.astype(BF16), k16) * decay).astype(BF16)
        glast = jnp.concatenate([jnp.broadcast_to(G[(c + 1) * C - 1:(c + 1) * C, :], (C, LANES)) for c in range(nc)],
                                axis=0)
        M = -A
        return dict(M=M, T=eye + M, Ms=_split2(M), rhs16=rhs16, QK16=QK16, qg16=(q * eG).astype(BF16),
                    kg16=(k * jnp.exp(glast - G)).astype(BF16),
                    gl=[jnp.exp(G[(c + 1) * C - 1:(c + 1) * C, :]) for c in range(nc)])

    def square(s, it):
        s["M"] = _mm_split(s["Ms"], s["Ms"]) if it < NEUMANN_SPLIT_STEPS else _mm(s["Ms"][0], s["Ms"][0])

    def extend(s, it):
        if it < NEUMANN_SPLIT_STEPS:
            s["Ms"] = _split2(s["M"])
            s["T"] = s["T"] + _mm_split(_split2(s["T"]), s["Ms"])
        else:
            s["Ms"] = (s["M"].astype(BF16),)
            s["T"] = s["T"] + _mm(s["T"].astype(BF16), s["Ms"][0])

    def apply_inverse(h, s):
        UW = _mm(s["T"].astype(BF16), s["rhs16"])
        s["U"] = UW[:, 0:C_VAL_DIM]
        s["W16"] = UW[:, C_VAL_DIM:].astype(BF16)
        s["S"] = s_sc[h]
        s["vn"], s["oi"] = [], []

    n_sq = max(0, (min(nv, C) - 1).bit_length() - 1)
    st = [prepare(h) for h in heads]
    for it in range(n_sq):
        for s in st:
            square(s, it)
        for s in st:
            extend(s, it)
    for h in heads:
        apply_inverse(h, st[h])

    for c in range(nc):
        rs = slice(c * C, (c + 1) * C)
        for s in st:
            S16 = s["S"].astype(BF16)
            vn = s["U"][rs] - _mm(s["W16"][rs], S16)
            s["oi"].append(_mm(s["qg16"][rs], S16))
            s["S"] = s["S"] * s["gl"][c] + lax.dot_general(s["kg16"][rs], vn.astype(BF16), TN_DIMS,
                                                           preferred_element_type=F32)
            s["vn"].append(vn)
    for h, s in zip(heads, st):
        s_sc[h] = s["S"]
        s1_ref[h] = s["S"]
        o = jnp.concatenate(s["oi"], axis=0) + _mm(s["QK16"], jnp.concatenate(s["vn"], axis=0).astype(BF16))
        o = o * lax.rsqrt(jnp.mean(o * o, axis=-1, keepdims=True) + EPS) * nw_ref[...]
        o_ref[:, hsl[h]] = o[0:tb] * _silu(z_ref[:, hsl[h]].astype(F32))


def _gdn(uf3, uh3, conv0, s0, cw, alog_e, dtb_e, nw, tb, l_valid):
    Bn, Lp, _ = uf3.shape
    rows = max(tb, DELTA_CHUNK)
    const2 = lambda b, t: (0, 0)
    return pl.pallas_call(
        functools.partial(_gdn_kernel, tb=tb, l_valid=l_valid),
        grid=(Bn, Lp // tb),
        in_specs=[pl.BlockSpec((None, tb, C_CONV_DIM), lambda b, t: (b, t, HB_QKVC * COL // C_CONV_DIM)),
                  pl.BlockSpec((None, tb, COL), lambda b, t: (b, t, FB_BETA)),
                  pl.BlockSpec((None, tb, COL), lambda b, t: (b, t, FB_ALPHA)),
                  pl.BlockSpec((None, tb, COL), lambda b, t: (b, t, HB_ZC)),
                  pl.BlockSpec((None, CONV_WIDTH - 1, C_CONV_DIM), lambda b, t: (b, 0, 0)),
                  pl.BlockSpec((None, C_HEADS, C_KEY_DIM, C_VAL_DIM), lambda b, t: (b, 0, 0, 0)),
                  pl.BlockSpec((CONV_WIDTH, C_CONV_DIM), const2),
                  pl.BlockSpec((1, C_V_WIDTH), const2),
                  pl.BlockSpec((1, C_V_WIDTH), const2),
                  pl.BlockSpec((1, C_VAL_DIM), const2)],
        out_specs=[pl.BlockSpec((None, tb, C_V_WIDTH), lambda b, t: (b, t, 0)),
                   pl.BlockSpec((None, CONV_WIDTH - 1, C_CONV_DIM), lambda b, t: (b, 0, 0)),
                   pl.BlockSpec((None, C_HEADS, C_KEY_DIM, C_VAL_DIM), lambda b, t: (b, 0, 0, 0))],
        out_shape=[jax.ShapeDtypeStruct((Bn, Lp, C_V_WIDTH), F32),
                   jax.ShapeDtypeStruct((Bn, CONV_WIDTH - 1, C_CONV_DIM), F32),
                   jax.ShapeDtypeStruct((Bn, C_HEADS, C_KEY_DIM, C_VAL_DIM), F32)],
        scratch_shapes=[pltpu.VMEM((SUBLANES + tb, C_CONV_DIM), F32),
                        pltpu.VMEM((C_HEADS, C_KEY_DIM, C_VAL_DIM), F32),
                        pltpu.VMEM((rows, C_CONV_DIM), F32)],
        compiler_params=_cparams("parallel", "arbitrary"),
        name="gdn",
    )(uh3, uf3, uf3, uh3, conv0, s0, cw, alog_e, dtb_e, nw.reshape(1, C_VAL_DIM))


def _merge_kernel(h_ref, oa_ref, za_ref, ob_ref, oc_ref, g0_ref, g1_ref, g2_ref,
                  wpa_ref, wpb_ref, wpc_ref, wout_ref, fnw_ref, hn_ref, *, final_norm):
    def proj(x, w_ref):
        return jnp.dot(x.astype(BF16), w_ref[...], preferred_element_type=F32)

    f32 = lambda ref: ref[...].astype(F32)
    ya = proj(_silu(f32(za_ref)) * oa_ref[...], wpa_ref)
    yb = proj(ob_ref[...], wpb_ref)
    yc = proj(oc_ref[...], wpc_ref)
    merged = _sigmoid(f32(g0_ref)) * ya + _sigmoid(f32(g1_ref)) * yb + _sigmoid(f32(g2_ref)) * yc
    hn = h_ref[...] + proj(merged, wout_ref)
    if final_norm:
        hn = hn * lax.rsqrt(jnp.mean(hn * hn, axis=-1, keepdims=True) + EPS) * fnw_ref[...]
    hn_ref[...] = hn


def _merge(h2d, oa, uh, ob, oc, wpa, wpb, wpc, wout, fnw, tm, final_norm):
    T = h2d.shape[0]
    gpb = D_MODEL // COL
    tok = lambda w: pl.BlockSpec((tm, w), lambda i: (i, 0))
    wspec = lambda k: pl.BlockSpec((k, D_MODEL), lambda i: (0, 0))
    gate_spec = lambda g: pl.BlockSpec((tm, D_MODEL), lambda i: (i, HB_GATES // gpb + g))
    return pl.pallas_call(
        functools.partial(_merge_kernel, final_norm=final_norm),
        grid=(T // tm,),
        in_specs=[tok(D_MODEL), tok(A_WIDTH),
                  pl.BlockSpec((tm, COL), lambda i: (i, HB_ZA)),
                  tok(B_WIDTH), tok(C_V_WIDTH),
                  gate_spec(0), gate_spec(1), gate_spec(2),
                  wspec(A_WIDTH), wspec(B_WIDTH), wspec(C_V_WIDTH), wspec(D_MODEL),
                  pl.BlockSpec((1, D_MODEL), lambda i: (0, 0))],
        out_specs=tok(D_MODEL),
        out_shape=jax.ShapeDtypeStruct((T, D_MODEL), F32),
        compiler_params=_cparams("parallel"),
        name="merge",
    )(h2d, oa, uh, ob, oc, uh, uh, uh, wpa, wpb, wpc, wout, fnw.reshape(1, D_MODEL))


def _block_diag(w):
    g, n, _ = w.shape
    same = jnp.eye(g, dtype=bool)[:, None, :, None]
    return jnp.where(same, w[:, :, None, :], 0.0).reshape(g * n, g * n)


def _rearrange_w_in(w):
    sizes = (A_WIDTH, A_WIDTH, A_WIDTH, A_WIDTH, B_WIDTH, B_WIDTH, C_CONV_DIM, C_V_WIDTH, C_HEADS, C_HEADS,
             N_BRANCH * D_MODEL)
    offs = [0]
    for s in sizes:
        offs.append(offs[-1] + s)
    qa, ka, va, za, xb, zb, qkvc, zc, beta, alpha, gates = (w[:, offs[n]:offs[n + 1]] for n in range(len(sizes)))
    rep = lambda c: jnp.repeat(c, C_VAL_DIM, axis=1)
    return jnp.concatenate([qa, ka, va, rep(beta), rep(alpha), qkvc, za, gates, xb, zb, zc], axis=1).astype(BF16)


def kernel(x_prompt, x_sample, cache_k, cache_v, page_table, state_conv_b, state_h_b, state_conv_c, state_s_c,
           norm_w, w_in, conv_b_w, conv_b_b, lru_wa, lru_ba, lru_wx, lru_bx, lru_lambda,
           conv_c_w, dn_a_log, dn_dt_bias, dn_norm_w, w_pa, w_pb, w_pc, w_out, final_norm_w):
    Bp, L, _ = x_prompt.shape
    Bs, dec_seq, _ = x_sample.shape
    depth = w_in.shape[0]
    n_pool = cache_k.shape[1]
    n_pages = page_table.shape[1]
    past = n_pages * PAGE_SIZE
    tpad = BF16_ROWS
    assert dec_seq <= tpad and L % MOBA_BLOCK == 0 and past % MOBA_BLOCK == 0 and n_pages % PAGES_PER_STEP == 0
    assert MOBA_TOPK <= past // MOBA_BLOCK <= PAGE_SIZE and (dec_seq * A_HEADS) % BF16_ROWS == 0

    slopes = jnp.exp2(-8.0 * jnp.arange(1, A_HEADS + 1, dtype=F32) / A_HEADS)
    row = jnp.arange(dec_seq * A_HEADS)
    slope_rows = jnp.broadcast_to(slopes[row % A_HEADS][:, None], (row.size, LANES))
    tq_rows = jnp.broadcast_to((past + row // A_HEADS).astype(F32)[:, None], (row.size, LANES))
    ck_t = jnp.transpose(cache_k, (0, 1, 3, 4, 2)).reshape(depth, n_pool, A_WIDTH, PAGE_SIZE)
    cv_t = jnp.transpose(cache_v, (0, 1, 3, 4, 2)).reshape(depth, n_pool, A_WIDTH, PAGE_SIZE)

    hp = x_prompt.reshape(Bp * L, D_MODEL)
    hs = jnp.pad(x_sample, ((0, 0), (0, tpad - dec_seq), (0, 0))).reshape(Bs * tpad, D_MODEL)
    zeros = lambda *s: jnp.zeros(s, F32)
    tm_p = 2048 if (Bp * L) % 2048 == 0 else MOBA_BLOCK
    tm_merge = min(tm_p, 512)
    tm_s = Bs * tpad
    tc_p = 256 if L % 256 == 0 else L
    outs = [[] for _ in range(12)]
    for l in range(depth):
        w_re = _rearrange_w_in(w_in[l])
        wa_bd = _block_diag(lru_wa[l]).astype(BF16)
        wx_bd = _block_diag(lru_wx[l]).astype(BF16)
        alog_e = jnp.repeat(dn_a_log[l], C_VAL_DIM).reshape(1, C_V_WIDTH)
        dtb_e = jnp.repeat(dn_dt_bias[l], C_VAL_DIM).reshape(1, C_V_WIDTH)
        wpa, wpb, wpc, wout = (w.astype(BF16) for w in (w_pa[l], w_pb[l], w_pc[l], w_out[l]))
        last = l == depth - 1
        lru_w = (conv_b_w[l], conv_b_b[l], wa_bd, lru_ba[l], wx_bd, lru_bx[l], lru_lambda[l])
        gdn_w = (conv_c_w[l], alog_e, dtb_e, dn_norm_w[l])

        uf, uh = _inproj(hp, norm_w[l], w_re, tm_p)
        uf3 = uf.reshape(Bp, L, N_UF)
        uh3 = uh.reshape(Bp, L, N_UH)
        usf, ush = _inproj(hs, norm_w[l], w_re, tm_s)
        oa, oas = _moba(uf, usf, slopes, slope_rows, tq_rows, ck_t, cv_t, page_table, l, Bp, L, Bs, tpad, dec_seq)
        ob, cbp, hbp = _lru(uh3, zeros(Bp, CONV_WIDTH - 1, B_WIDTH), zeros(Bp, B_WIDTH), *lru_w, tc_p, L)
        oc, ccp, scp = _gdn(uf3, uh3, zeros(Bp, CONV_WIDTH - 1, C_CONV_DIM),
                            zeros(Bp, C_HEADS, C_KEY_DIM, C_VAL_DIM), *gdn_w, tc_p, L)
        hp = _merge(hp, oa, uh, ob.reshape(Bp * L, B_WIDTH), oc.reshape(Bp * L, C_V_WIDTH),
                    wpa, wpb, wpc, wout, final_norm_w, tm_merge, last)
        kp = uf3[:, :, FB_KA * COL:(FB_KA + 1) * COL].reshape(Bp, L, A_HEADS, A_HEAD_DIM)
        vp = uf3[:, :, FB_VA * COL:(FB_VA + 1) * COL].reshape(Bp, L, A_HEADS, A_HEAD_DIM)

        usf3 = usf.reshape(Bs, tpad, N_UF)
        ush3 = ush.reshape(Bs, tpad, N_UH)
        obs, cbs, hbs = _lru(ush3, state_conv_b[l], state_h_b[l], *lru_w, tpad, dec_seq)
        ocs, ccs, scs = _gdn(usf3, ush3, state_conv_c[l], state_s_c[l], *gdn_w, tpad, dec_seq)
        hs = _merge(hs, oas, ush, obs.reshape(Bs * tpad, B_WIDTH), ocs.reshape(Bs * tpad, C_V_WIDTH),
                    wpa, wpb, wpc, wout, final_norm_w, tm_s, last)
        ks = usf3[:, :dec_seq, FB_KA * COL:(FB_KA + 1) * COL].reshape(Bs, dec_seq, A_HEADS, A_HEAD_DIM)
        vs = usf3[:, :dec_seq, FB_VA * COL:(FB_VA + 1) * COL].reshape(Bs, dec_seq, A_HEADS, A_HEAD_DIM)

        hbp = hbp.reshape(Bp, B_WIDTH)
        hbs = hbs.reshape(Bs, B_WIDTH)
        for lst, val in zip(outs, (kp, vp, ks, vs, cbp, hbp, cbs, hbs, ccp, scp, ccs, scs)):
            lst.append(val)

    y_prompt = hp.reshape(Bp, L, D_MODEL)
    y_sample = hs.reshape(Bs, tpad, D_MODEL)[:, :dec_seq]
    return (y_prompt, y_sample) + tuple(jnp.stack(o) for o in outs)
```

```python
import functools

import jax
import jax.numpy as jnp
from jax import lax
from jax.experimental import pallas as pl
from jax.experimental.pallas import tpu as pltpu

F32 = jnp.float32
BF16 = jnp.bfloat16
HIGHEST = lax.Precision.HIGHEST

D_MODEL = 1024
A_HEADS = 8
A_HEAD_DIM = 64
A_WIDTH = A_HEADS * A_HEAD_DIM
MOBA_BLOCK = 256
MOBA_TOPK = 3
PAGE_SIZE = 128
B_WIDTH = 512
B_BLOCKS = 8
LRU_C = 8.0
CONV_WIDTH = 4
C_HEADS = 4
C_KEY_DIM = 128
C_VAL_DIM = 128
C_QK_WIDTH = C_HEADS * C_KEY_DIM
C_V_WIDTH = C_HEADS * C_VAL_DIM
C_CONV_DIM = 2 * C_QK_WIDTH + C_V_WIDTH
DELTA_CHUNK = 64
NEUMANN_SPLIT_STEPS = 1
N_BRANCH = 3
EPS = 1e-6

COL = 512
FB_QA, FB_KA, FB_VA, FB_BETA, FB_ALPHA = 0, 1, 2, 3, 4
N_FBLK = 5
HB_QKVC, HB_ZA, HB_GATES, HB_XB, HB_ZB, HB_ZC = 0, 3, 4, 10, 11, 12
N_HBLK = 13
N_UF = N_FBLK * COL
N_UH = N_HBLK * COL

SUBLANES = 8
LANES = 128
BF16_ROWS = 16
TAIL0 = SUBLANES - (CONV_WIDTH - 1)
VMEM_LIMIT = 56 * 1024 * 1024

NT_DIMS = (((1,), (1,)), ((), ()))
TN_DIMS = (((0,), (0,)), ((), ()))


def _cparams(*sem):
    return pltpu.CompilerParams(dimension_semantics=sem, vmem_limit_bytes=VMEM_LIMIT)


def _sigmoid(x):
    return 0.5 * jnp.tanh(0.5 * x) + 0.5


def _silu(x):
    return x * _sigmoid(x)


def _softplus(x):
    return jnp.maximum(x, 0.0) + jnp.log1p(jnp.exp(-jnp.abs(x)))


def _mm(a, b):
    return jnp.dot(a, b, preferred_element_type=F32)


def _mm_nt(a, b):
    return lax.dot_general(a, b, NT_DIMS, preferred_element_type=F32)


def _split2(x):
    hi = x.astype(BF16)
    lo = (x - hi.astype(F32)).astype(BF16)
    return hi, lo


def _mm_split(a, b):
    (ah, al), (bh, bl) = a, b
    return _mm(ah, bh) + (_mm(ah, bl) + _mm(al, bh))


def _inproj_kernel(h_ref, nw_ref, w_ref, uf_ref, uh_ref, xn_sc):
    j = pl.program_id(1)

    @pl.when(j == 0)
    def _():
        x = h_ref[...]
        y = x * lax.rsqrt(jnp.mean(x * x, axis=-1, keepdims=True) + EPS) * nw_ref[...]
        xn_sc[...] = y.astype(BF16)

    def block():
        return jnp.dot(xn_sc[...], w_ref[...], preferred_element_type=F32)

    @pl.when(j < N_FBLK)
    def _():
        uf_ref[...] = block()

    @pl.when(j >= N_FBLK)
    def _():
        uh_ref[...] = block().astype(BF16)


def _inproj(h2d, norm_w, w_re, tm):
    T = h2d.shape[0]
    return pl.pallas_call(
        _inproj_kernel,
        grid=(T // tm, N_FBLK + N_HBLK),
        in_specs=[pl.BlockSpec((tm, D_MODEL), lambda i, j: (i, 0)),
                  pl.BlockSpec((1, D_MODEL), lambda i, j: (0, 0)),
                  pl.BlockSpec((D_MODEL, COL), lambda i, j: (0, j))],
        out_specs=[pl.BlockSpec((tm, COL), lambda i, j: (i, jnp.minimum(j, N_FBLK - 1))),
                   pl.BlockSpec((tm, COL), lambda i, j: (i, jnp.maximum(j - N_FBLK, 0)))],
        out_shape=[jax.ShapeDtypeStruct((T, N_UF), F32), jax.ShapeDtypeStruct((T, N_UH), BF16)],
        scratch_shapes=[pltpu.VMEM((tm, D_MODEL), BF16)],
        compiler_params=_cparams("parallel", "arbitrary"),
        name="inproj",
    )(h2d, norm_w.reshape(1, D_MODEL), w_re)


TILES_WIDE = 4
TILES_NARROW = 2

def _topk_select(gate, n_past, n_iota):
    nb = gate.shape[0]
    g = jnp.where(n_iota < n_past, gate, -jnp.inf)
    rank = jnp.zeros_like(g)
    for m in range(min(nb, n_past) if isinstance(n_past, int) else nb):
        gm = g[m:m + 1, :]
        rank = rank + jnp.where(n_iota > m, jnp.where(gm >= g, 1.0, 0.0), jnp.where(gm > g, 1.0, 0.0))
    return jnp.where(n_iota < n_past, jnp.where(rank < MOBA_TOPK, 1.0, 0.0), 0.0)


def _moba_prompt_body(pair, i, slopes_ref, q_ref, k_ref, v_ref, o_ref, kb_sc, vt_sc, means_sc, cmask_sc, sel_sc, s_sc,
                      qt_sc, *, nblk):
    BLK, HD, PW = MOBA_BLOCK, A_HEAD_DIM, 2 * A_HEAD_DIM
    lane = lax.broadcasted_iota(jnp.int32, (1, PW), 1)

    @pl.when(i == 0)
    def _init():
        c_col = lax.broadcasted_iota(jnp.int32, (BLK, PW), 0).astype(F32)
        lane_a = lax.broadcasted_iota(jnp.int32, (BLK, PW), 1)
        key_off = jnp.where(lane_a == 0, c_col, 0.0).astype(BF16)
        ones = jnp.ones((BF16_ROWS, BLK), BF16)
        for n in range(nblk):
            kblk = k_ref[n * BLK:(n + 1) * BLK, :]
            mean = jnp.sum(kblk, axis=0, keepdims=True) * (1.0 / BLK)
            means_sc[n:n + 1, :] = jnp.where(lane < HD, mean, 0.0)
            means_sc[nblk + n:nblk + n + 1, :] = jnp.where(lane < HD, 0.0, mean)
            kb_sc[n, :, 0:PW] = kblk.astype(BF16)
            kb_sc[n, :, PW:2 * PW] = key_off
            vt = v_ref[n * BLK:(n + 1) * BLK, :].T
            for hh in range(2):
                vt_sc[n, hh, 0:HD, :] = vt[hh * HD:(hh + 1) * HD, :].astype(BF16)
                vt_sc[n, hh, HD:HD + BF16_ROWS, :] = ones
        c_io = lax.broadcasted_iota(jnp.int32, (BLK, BLK), 0)
        r_io = lax.broadcasted_iota(jnp.int32, (BLK, BLK), 1)
        cmask_sc[...] = jnp.where(c_io <= r_io, 0.0, -jnp.inf)
        n_iota = lax.broadcasted_iota(jnp.int32, (nblk, BLK), 0)
        for ib in range(nblk):
            qt_b = (q_ref[ib * BLK:(ib + 1) * BLK, :] * (HD ** -0.5)).T
            qt_sc[ib] = qt_b
            gate2 = jnp.dot(means_sc[...], qt_b, precision=HIGHEST, preferred_element_type=F32)
            for hh in range(2):
                sel_sc[hh, ib] = _topk_select(gate2[hh * nblk:(hh + 1) * nblk, :], ib, n_iota)

    qt = qt_sc[i]
    dim_row = lax.broadcasted_iota(jnp.int32, (PW, BLK), 0)
    qaugs, slopes = [], []
    for hh in range(2):
        slope = slopes_ref[pair * 2 + hh]
        qh = jnp.where((dim_row // HD) == hh, qt, 0.0).astype(BF16)
        slope_row = jnp.where(dim_row == 0, slope, 0.0).astype(BF16)
        qaugs.append(jnp.concatenate([qh, slope_row], axis=0))
        slopes.append(slope)

    def colmax8(s):
        out = s[0:SUBLANES]
        for r in range(1, BLK // SUBLANES):
            out = jnp.maximum(out, s[r * SUBLANES:(r + 1) * SUBLANES])
        return out

    def past_tile(t, hh):
        j = jnp.minimum(t, i - 1)
        on = sel_sc[hh, i, pl.ds(j, 1), :] * jnp.where(t < i, 1.0, 0.0) > 0.0
        delta = -slopes[hh] * ((i - j) * BLK).astype(F32)
        return j, on, delta

    s_own = [_mm(kb_sc[i], qaugs[hh]) + cmask_sc[...] for hh in range(2)]
    n_wide = i // TILES_WIDE
    t_narrow = n_wide * TILES_WIDE
    n_narrow = (i - t_narrow + TILES_NARROW - 1) // TILES_NARROW

    def max_trip(width, t0):
        def trip(tt, mx):
            mx = list(mx)
            for e in range(width):
                for hh in range(2):
                    j, on, delta = past_tile(t0 + width * tt + e, hh)
                    s = _mm(kb_sc[j], qaugs[hh])
                    s_sc[hh, j] = s
                    mx[hh] = jnp.maximum(mx[hh], jnp.where(on, colmax8(s) + delta, -jnp.inf))
            return tuple(mx)
        return trip

    mx = lax.fori_loop(0, n_wide, max_trip(TILES_WIDE, 0), tuple(colmax8(s) for s in s_own))
    mx = lax.fori_loop(0, n_narrow, max_trip(TILES_NARROW, t_narrow), mx)
    m = [jnp.max(x, axis=0, keepdims=True) for x in mx]

    def pv_trip(width, t0):
        def trip(tt, acc):
            acc = list(acc)
            for hh in range(2):
                ps, vts = [], []
                for e in range(width):
                    j, on, delta = past_tile(t0 + width * tt + e, hh)
                    p = jnp.exp(s_sc[hh, j] - jnp.where(on, m[hh] - delta, jnp.inf))
                    ps.append(p.astype(BF16))
                    vts.append(vt_sc[j, hh])
                acc[hh] = acc[hh] + _mm(jnp.concatenate(vts, axis=1), jnp.concatenate(ps, axis=0))
            return tuple(acc)
        return trip

    acc0 = tuple(_mm(vt_sc[i, hh], jnp.exp(s_own[hh] - m[hh]).astype(BF16)) for hh in range(2))
    acc = lax.fori_loop(0, n_wide, pv_trip(TILES_WIDE, 0), acc0)
    acc = lax.fori_loop(0, n_narrow, pv_trip(TILES_NARROW, t_narrow), acc)
    outs = [a[0:HD, :] / a[HD:HD + 1, :] for a in acc]
    o_ref[...] = jnp.concatenate(outs, axis=0).T


PAGES_PER_STEP = 16


def _moba_sample_body(active, ph, st, slope_ref, tq_ref, q_ref, kn_ref, vn_ref, k_refs, v_refs, o_ref,
                      qbd_sc, s_sc, p_sc, acc_sc, linv_sc, *, n_pages, dec_seq):
    PPS = PAGES_PER_STEP
    nsteps = n_pages // PPS
    ppb = MOBA_BLOCK // PAGE_SIZE
    nb = n_pages // ppb
    NR = dec_seq * A_HEADS
    head_of_row = lax.broadcasted_iota(jnp.int32, (A_HEADS, A_WIDTH), 0)
    head_of_lane = lax.broadcasted_iota(jnp.int32, (A_HEADS, A_WIDTH), 1) // A_HEAD_DIM
    hmask = jnp.where(head_of_row == head_of_lane, 1.0, 0.0)

    @pl.when(active & (ph == 0) & (st == 0))
    def _init():
        for t in range(dec_seq):
            qrow = q_ref[t:t + 1, :] * (A_HEAD_DIM ** -0.5)
            qbd_sc[t * A_HEADS:(t + 1) * A_HEADS, :] = qrow * hmask

    @pl.when(active & (ph == 0))
    def _scores():
        qbd = qbd_sc[...].astype(BF16)
        for r in range(PPS):
            s_sc[st * PPS + r] = _mm(qbd, k_refs[r][...].astype(BF16))

    @pl.when(active & (ph == 0) & (st == nsteps - 1))
    def _softmax():
        slope = slope_ref[...]
        tq = tq_ref[...]
        lane = lax.broadcasted_iota(jnp.int32, (NR, PAGE_SIZE), 1)
        lane_f = lane.astype(F32)
        q_of_row = lax.broadcasted_iota(jnp.int32, (NR, PAGE_SIZE), 0) // A_HEADS
        gate = jnp.full((NR, PAGE_SIZE), -jnp.inf, F32)
        for n in range(nb):
            tot = s_sc[ppb * n]
            for e in range(1, ppb):
                tot = tot + s_sc[ppb * n + e]
            gate = jnp.where(lane == n, jnp.sum(tot, axis=-1, keepdims=True) * (1.0 / MOBA_BLOCK), gate)
        sel = jnp.zeros((NR, PAGE_SIZE), F32)
        for _ in range(MOBA_TOPK):
            mx = jnp.max(gate, axis=-1, keepdims=True)
            first = jnp.min(jnp.where(gate == mx, lane_f, float(PAGE_SIZE)), axis=-1, keepdims=True)
            pick = lane_f == first
            sel = jnp.where(pick, 1.0, sel)
            gate = jnp.where(pick, -jnp.inf, gate)
        pad = jnp.zeros((PAGE_SIZE - kn_ref.shape[0], A_WIDTH), F32)
        qbd = qbd_sc[...].astype(BF16)
        s_own = _mm_nt(qbd, jnp.concatenate([kn_ref[...], pad], axis=0).astype(BF16))
        own_ok = (lane <= q_of_row) & (lane < dec_seq)
        s_own = jnp.where(own_ok, s_own - slope * (q_of_row - lane).astype(F32), -jnp.inf)
        m = jnp.max(s_own, axis=-1, keepdims=True)
        for n in range(nb):
            on = jnp.sum(jnp.where(lane == n, sel, 0.0), axis=-1, keepdims=True) > 0.0
            for e in range(ppb):
                pg = ppb * n + e
                sb = s_sc[pg] - slope * (tq - (lane_f + float(pg * PAGE_SIZE)))
                sb = jnp.where(on, sb, -jnp.inf)
                s_sc[pg] = sb
                m = jnp.maximum(m, jnp.max(sb, axis=-1, keepdims=True))
        p_own = jnp.exp(s_own - m)
        l = jnp.sum(p_own, axis=-1, keepdims=True)
        for pg in range(n_pages):
            pb = jnp.exp(s_sc[pg] - m)
            l = l + jnp.sum(pb, axis=-1, keepdims=True)
            p_sc[pg] = pb.astype(BF16)
        linv_sc[...] = jnp.broadcast_to(1.0 / l, (NR, LANES))
        v_own = jnp.concatenate([vn_ref[...], pad], axis=0).astype(BF16)
        acc_sc[...] = _mm(p_own.astype(BF16), v_own)

    @pl.when(active & (ph == 1))
    def _values():
        acc = acc_sc[...]
        for r in range(PPS):
            acc = acc + _mm_nt(p_sc[st * PPS + r], v_refs[r][...].astype(BF16))
        acc_sc[...] = acc

    @pl.when(active & (ph == 1) & (st == nsteps - 1))
    def _finish():
        linv = jnp.concatenate([linv_sc[...]] * (A_WIDTH // LANES), axis=1)
        a = acc_sc[...] * linv
        o_ref[...] = jnp.zeros_like(o_ref)
        for t in range(dec_seq):
            o_ref[t:t + 1, :] = jnp.sum(a[t * A_HEADS:(t + 1) * A_HEADS, :] * hmask, axis=0, keepdims=True)


N_PROMPT_SCRATCH = 7
N_PROMPT_IN = 4
N_SAMPLE_IN = 5


def _sample_item(step, n_items, nsteps):
    w = jnp.minimum(step, n_items - 1)
    return step < n_items, w // (2 * nsteps), (w // nsteps) % 2, w % nsteps


def _moba_kernel(tb_ref, *refs, nblk, n_pages, dec_seq, n_items, layer):
    PPS = PAGES_PER_STEP
    nsteps = n_pages // PPS
    prompt_in = refs[:N_PROMPT_IN]
    sample_in = refs[N_PROMPT_IN:N_PROMPT_IN + N_SAMPLE_IN]
    ck_ref, cv_ref = refs[N_PROMPT_IN + N_SAMPLE_IN:N_PROMPT_IN + N_SAMPLE_IN + 2]
    o_prompt, o_sample = refs[N_PROMPT_IN + N_SAMPLE_IN + 2:N_PROMPT_IN + N_SAMPLE_IN + 4]
    scratch = refs[N_PROMPT_IN + N_SAMPLE_IN + 4:]
    page_buf, page_sem = scratch[-2:]
    b, pair, i = pl.program_id(0), pl.program_id(1), pl.program_id(2)
    step = (b * pl.num_programs(1) + pair) * nblk + i

    def page_copy(cache_ref, s, r):
        page = tb_ref[s * (PPS + 1) + r]
        return pltpu.make_async_copy(cache_ref.at[layer, page], page_buf.at[s % 2, r], page_sem.at[s % 2])

    def batch(s, act):
        phase = _sample_item(s, n_items, nsteps)[2]

        @pl.when(phase == 0)
        def _():
            for r in range(PPS):
                act(page_copy(ck_ref, s, r))

        @pl.when(phase == 1)
        def _():
            for r in range(PPS):
                act(page_copy(cv_ref, s, r))

    @pl.when(step == 0)
    def _():
        batch(step, lambda cp: cp.start())

    @pl.when(step + 1 < n_items)
    def _():
        batch(step + 1, lambda cp: cp.start())

    _moba_prompt_body(pair, i, *prompt_in, o_prompt, *scratch[:N_PROMPT_SCRATCH], nblk=nblk)

    active, _, ph, st = _sample_item(step, n_items, nsteps)

    @pl.when(active)
    def _():
        batch(step, lambda cp: cp.wait())

    pages = [page_buf.at[step % 2, r] for r in range(PPS)]
    _moba_sample_body(active, ph, st, *sample_in, pages, pages, o_sample,
                      *scratch[N_PROMPT_SCRATCH:-2], n_pages=n_pages, dec_seq=dec_seq)


def _moba(uf, usf, slopes, slope_rows, tq_rows, ck_t, cv_t, page_table, layer, Bp, L, Bs, tpad, dec_seq):
    nblk = L // MOBA_BLOCK
    PW = 2 * A_HEAD_DIM
    cpb = COL // PW
    npair = A_HEADS // 2
    n_pages = page_table.shape[1]
    PPS = PAGES_PER_STEP
    nsteps = n_pages // PPS
    n_items = Bs * 2 * nsteps
    assert n_items <= Bp * npair * nblk, "more sample page batches than prompt grid steps"
    NR = dec_seq * A_HEADS
    us3 = usf.reshape(Bs, tpad, N_UF)

    n_grid = Bp * npair * nblk
    _, seq, _, st = _sample_item(jnp.arange(n_grid), n_items, nsteps)
    pages = page_table[seq[:, None], (st * PPS)[:, None] + jnp.arange(PPS)[None, :]]
    width = PPS + 1
    table = jnp.concatenate([pages, seq[:, None]], axis=1).astype(jnp.int32).reshape(n_grid * width)

    def seq_map(col):
        return lambda b, p, i, tb: (tb[((b * npair + p) * nblk + i) * width + PPS], 0, col)

    in_specs = [pl.BlockSpec(memory_space=pltpu.SMEM),
                pl.BlockSpec((L, PW), lambda b, p, i, pt: (b, FB_QA * cpb + p)),
                pl.BlockSpec((L, PW), lambda b, p, i, pt: (b, FB_KA * cpb + p)),
                pl.BlockSpec((L, PW), lambda b, p, i, pt: (b, FB_VA * cpb + p)),
                pl.BlockSpec((NR, LANES), lambda b, p, i, pt: (0, 0)),
                pl.BlockSpec((NR, LANES), lambda b, p, i, pt: (0, 0)),
                pl.BlockSpec((None, tpad, COL), seq_map(FB_QA)),
                pl.BlockSpec((None, tpad, COL), seq_map(FB_KA)),
                pl.BlockSpec((None, tpad, COL), seq_map(FB_VA))]
    in_specs += [pl.BlockSpec(memory_space=pl.ANY), pl.BlockSpec(memory_space=pl.ANY)]
    grid_spec = pltpu.PrefetchScalarGridSpec(
        num_scalar_prefetch=1,
        grid=(Bp, npair, nblk),
        in_specs=in_specs,
        out_specs=[pl.BlockSpec((MOBA_BLOCK, PW), lambda b, p, i, pt: (b * nblk + i, p)),
                   pl.BlockSpec((None, tpad, A_WIDTH), seq_map(0))],
        scratch_shapes=[pltpu.VMEM((nblk, MOBA_BLOCK, 2 * PW), BF16),
                        pltpu.VMEM((nblk, 2, A_HEAD_DIM + BF16_ROWS, MOBA_BLOCK), BF16),
                        pltpu.VMEM((2 * nblk, PW), F32),
                        pltpu.VMEM((MOBA_BLOCK, MOBA_BLOCK), F32),
                        pltpu.VMEM((2, nblk, nblk, MOBA_BLOCK), F32),
                        pltpu.VMEM((2, nblk, MOBA_BLOCK, MOBA_BLOCK), F32),
                        pltpu.VMEM((nblk, PW, MOBA_BLOCK), F32),
                        pltpu.VMEM((NR, A_WIDTH), F32),
                        pltpu.VMEM((n_pages, NR, PAGE_SIZE), F32),
                        pltpu.VMEM((n_pages, NR, PAGE_SIZE), BF16),
                        pltpu.VMEM((NR, A_WIDTH), F32),
                        pltpu.VMEM((NR, LANES), F32),
                        pltpu.VMEM((2, PPS, A_WIDTH, PAGE_SIZE), F32),
                        pltpu.SemaphoreType.DMA((2,))])
    oa, oas = pl.pallas_call(
        functools.partial(_moba_kernel, nblk=nblk, n_pages=n_pages, dec_seq=dec_seq, n_items=n_items, layer=layer),
        grid_spec=grid_spec,
        out_shape=[jax.ShapeDtypeStruct((Bp * L, A_WIDTH), F32), jax.ShapeDtypeStruct((Bs, tpad, A_WIDTH), F32)],
        compiler_params=_cparams("arbitrary", "arbitrary", "arbitrary"),
        name="moba",
    )(table, slopes, uf, uf, uf, slope_rows, tq_rows, us3, us3, us3, ck_t, cv_t)
    return oa, oas.reshape(Bs * tpad, A_WIDTH)


def _lru_kernel(xb_ref, zb_ref, conv0_ref, h0_ref, cw_ref, cb_ref, wa_ref, ba_ref, wx_ref, bx_ref, lam_ref,
                y_ref, conv1_ref, h1_ref, xp_sc, h_sc, a_sc, b_sc, *, Bn, tc, l_valid):
    c = pl.program_id(0)

    @pl.when(c == 0)
    def _():
        xp_sc[:, TAIL0:SUBLANES, :] = conv0_ref[...]
        h_sc[...] = h0_ref[...]

    xp_sc[:, SUBLANES:SUBLANES + tc, :] = xb_ref[...].astype(F32)
    xc = xp_sc[:, TAIL0:TAIL0 + tc, :] * cw_ref[0:1, :]
    for j in range(1, CONV_WIDTH):
        xc = xc + xp_sc[:, TAIL0 + j:TAIL0 + j + tc, :] * cw_ref[j:j + 1, :]
    xc = xc + cb_ref[...]
    nv = min(tc, l_valid)
    tail = xp_sc[:, TAIL0 + nv:SUBLANES + nv, :]
    conv1_ref[...] = tail
    xp_sc[:, TAIL0:SUBLANES, :] = tail

    x2 = xc.reshape(Bn * tc, B_WIDTH)
    x16 = x2.astype(BF16)
    r = _sigmoid(jnp.dot(x16, wa_ref[...], preferred_element_type=F32) + ba_ref[...])
    ig = _sigmoid(jnp.dot(x16, wx_ref[...], preferred_element_type=F32) + bx_ref[...])
    log_a = -LRU_C * r * _softplus(-lam_ref[...])
    a = jnp.exp(log_a)
    bin_ = jnp.sqrt(-jnp.tanh(log_a) * (a * a + 1.0)) * (ig * x2)
    a_sc[...] = a.reshape(Bn, tc, B_WIDTH)
    b_sc[...] = bin_.reshape(Bn, tc, B_WIDTH)

    def step(t, h):
        h = a_sc[:, pl.ds(t, 1), :] * h + b_sc[:, pl.ds(t, 1), :]
        a_sc[:, pl.ds(t, 1), :] = h
        return h

    h = lax.fori_loop(0, nv, step, h_sc[...])
    h_sc[...] = h
    h1_ref[...] = h
    y_ref[...] = _silu(zb_ref[...].astype(F32)) * a_sc[...]


def _lru(u3, conv0, h0, cw, cb, wa_bd, ba, wx_bd, bx, lam, tc, l_valid):
    Bn, Lp, _ = u3.shape
    row = lambda v: v.reshape(1, B_WIDTH)
    const2 = lambda c: (0, 0)
    return pl.pallas_call(
        functools.partial(_lru_kernel, Bn=Bn, tc=tc, l_valid=l_valid),
        grid=(Lp // tc,),
        in_specs=[pl.BlockSpec((Bn, tc, COL), lambda c: (0, c, HB_XB)),
                  pl.BlockSpec((Bn, tc, COL), lambda c: (0, c, HB_ZB)),
                  pl.BlockSpec((Bn, CONV_WIDTH - 1, B_WIDTH), lambda c: (0, 0, 0)),
                  pl.BlockSpec((Bn, 1, B_WIDTH), lambda c: (0, 0, 0)),
                  pl.BlockSpec((CONV_WIDTH, B_WIDTH), const2),
                  pl.BlockSpec((1, B_WIDTH), const2),
                  pl.BlockSpec((B_WIDTH, B_WIDTH), const2),
                  pl.BlockSpec((1, B_WIDTH), const2),
                  pl.BlockSpec((B_WIDTH, B_WIDTH), const2),
                  pl.BlockSpec((1, B_WIDTH), const2),
                  pl.BlockSpec((1, B_WIDTH), const2)],
        out_specs=[pl.BlockSpec((Bn, tc, B_WIDTH), lambda c: (0, c, 0)),
                   pl.BlockSpec((Bn, CONV_WIDTH - 1, B_WIDTH), lambda c: (0, 0, 0)),
                   pl.BlockSpec((Bn, 1, B_WIDTH), lambda c: (0, 0, 0))],
        out_shape=[jax.ShapeDtypeStruct((Bn, Lp, B_WIDTH), F32),
                   jax.ShapeDtypeStruct((Bn, CONV_WIDTH - 1, B_WIDTH), F32),
                   jax.ShapeDtypeStruct((Bn, 1, B_WIDTH), F32)],
        scratch_shapes=[pltpu.VMEM((Bn, SUBLANES + tc, B_WIDTH), F32),
                        pltpu.VMEM((Bn, 1, B_WIDTH), F32),
                        pltpu.VMEM((Bn, tc, B_WIDTH), F32),
                        pltpu.VMEM((Bn, tc, B_WIDTH), F32)],
        compiler_params=_cparams("arbitrary"),
        name="rglru",
    )(u3, u3, conv0, h0.reshape(Bn, 1, B_WIDTH), cw, row(cb), wa_bd, row(ba), wx_bd, row(bx), row(lam))


def _gdn_kernel(qkv_ref, beta_ref, alpha_ref, z_ref, conv0_ref, s0_ref, cw_ref, alog_ref, dtb_ref, nw_ref,
                o_ref, conv1_ref, s1_ref, xp_sc, s_sc, y_sc, *, tb, l_valid):
    C = DELTA_CHUNK
    R = max(tb, C)
    nc = R // C
    t = pl.program_id(1)

    @pl.when(t == 0)
    def _():
        xp_sc[TAIL0:SUBLANES, :] = conv0_ref[...]
        s_sc[...] = s0_ref[...]

    xp_sc[SUBLANES:SUBLANES + tb, :] = qkv_ref[...].astype(F32)
    y = xp_sc[TAIL0:TAIL0 + tb, :] * cw_ref[0:1, :]
    for j in range(1, CONV_WIDTH):
        y = y + xp_sc[TAIL0 + j:TAIL0 + j + tb, :] * cw_ref[j:j + 1, :]
    nv = min(tb, l_valid)
    tail = xp_sc[TAIL0 + nv:SUBLANES + nv, :]
    conv1_ref[...] = tail
    xp_sc[TAIL0:SUBLANES, :] = tail
    y = _silu(y)

    if nv < R:
        live = lax.broadcasted_iota(jnp.int32, (tb, 1), 0) < nv
        y_sc[...] = jnp.zeros_like(y_sc)
        y_sc[0:tb, :] = jnp.where(live, y, 0.0)
    else:
        live = None
        y_sc[...] = y

    ri = lax.broadcasted_iota(jnp.int32, (R, R), 0)
    ci = lax.broadcasted_iota(jnp.int32, (R, R), 1)
    same = (ri // C) == (ci // C)
    causal = same & (ri >= ci)
    strict = same & (ri > ci)
    diag = ri == ci
    eye = jnp.where(diag, 1.0, 0.0)
    ltri16 = jnp.where(causal, 1.0, 0.0).astype(BF16)

    def padrows(v):
        if tb == R:
            return v
        return jnp.concatenate([v, jnp.zeros((R - tb, v.shape[1]), F32)], axis=0)

    def widen(v):
        return v[:, 0:R] if R <= LANES else jnp.concatenate([v] * (R // LANES), axis=1)

    heads = range(C_HEADS)
    hsl = [slice(h * C_KEY_DIM, (h + 1) * C_KEY_DIM) for h in heads]

    def prepare(h):
        q = y_sc[:, h * C_KEY_DIM:(h + 1) * C_KEY_DIM]
        k = y_sc[:, C_QK_WIDTH + h * C_KEY_DIM:C_QK_WIDTH + (h + 1) * C_KEY_DIM]
        v = y_sc[:, 2 * C_QK_WIDTH + h * C_VAL_DIM:2 * C_QK_WIDTH + (h + 1) * C_VAL_DIM]
        q = q * lax.rsqrt(jnp.sum(q * q, axis=-1, keepdims=True) + EPS) * (C_KEY_DIM ** -0.5)
        k = k * lax.rsqrt(jnp.sum(k * k, axis=-1, keepdims=True) + EPS)
        beta = _sigmoid(beta_ref[:, hsl[h]])
        g = -jnp.exp(alog_ref[:, hsl[h]]) * _softplus(alpha_ref[:, hsl[h]] + dtb_ref[:, hsl[h]])
        if live is not None:
            beta = jnp.where(live, beta, 0.0)
            g = jnp.where(live, g, 0.0)
        beta = padrows(beta)
        g = padrows(g)
        g_hi, g_lo = _split2(g)
        g_lo2 = (g - g_hi.astype(F32) - g_lo.astype(F32)).astype(BF16)
        G = _mm(ltri16, g_hi) + (_mm(ltri16, g_lo) + _mm(ltri16, g_lo2))
        Gcol = widen(G)
        Grow = jnp.sum(jnp.where(diag, Gcol, 0.0), axis=0, keepdims=True)
        decay = jnp.exp(jnp.where(causal, Gcol - Grow, -jnp.inf))
        kb = k * beta
        k16 = k.astype(BF16)
        A = jnp.where(strict, _mm_nt(kb.astype(BF16), k16) * decay, 0.0)
        eG = jnp.exp(G)
        rhs16 = jnp.concatenate([v * beta, kb * eG], axis=1).astype(BF16)
        QK16 = (_mm_nt(q.astype(BF16), k16) * decay).astype(BF16)
        glast = jnp.concatenate([jnp.broadcast_to(G[(c + 1) * C - 1:(c + 1) * C, :], (C, LANES)) for c in range(nc)],
                                axis=0)
        M = -A
        return dict(M=M, T=eye + M, Ms=_split2(M), rhs16=rhs16, QK16=QK16, qg16=(q * eG).astype(BF16),
                    kg16=(k * jnp.exp(glast - G)).astype(BF16),
                    gl=[jnp.exp(G[(c + 1) * C - 1:(c + 1) * C, :]) for c in range(nc)])

    def square(s, it):
        s["M"] = _mm_split(s["Ms"], s["Ms"]) if it < NEUMANN_SPLIT_STEPS else _mm(s["Ms"][0], s["Ms"][0])

    def extend(s, it):
        if it < NEUMANN_SPLIT_STEPS:
            s["Ms"] = _split2(s["M"])
            s["T"] = s["T"] + _mm_split(_split2(s["T"]), s["Ms"])
        else:
            s["Ms"] = (s["M"].astype(BF16),)
            s["T"] = s["T"] + _mm(s["T"].astype(BF16), s["Ms"][0])

    def apply_inverse(h, s):
        UW = _mm(s["T"].astype(BF16), s["rhs16"])
        s["U"] = UW[:, 0:C_VAL_DIM]
        s["W16"] = UW[:, C_VAL_DIM:].astype(BF16)
        s["S"] = s_sc[h]
        s["vn"], s["oi"] = [], []

    n_sq = max(0, (min(nv, C) - 1).bit_length() - 1)
    st = [prepare(h) for h in heads]
    for it in range(n_sq):
        for s in st:
            square(s, it)
        for s in st:
            extend(s, it)
    for h in heads:
        apply_inverse(h, st[h])

    for c in range(nc):
        rs = slice(c * C, (c + 1) * C)
        for s in st:
            S16 = s["S"].astype(BF16)
            vn = s["U"][rs] - _mm(s["W16"][rs], S16)
            s["oi"].append(_mm(s["qg16"][rs], S16))
            s["S"] = s["S"] * s["gl"][c] + lax.dot_general(s["kg16"][rs], vn.astype(BF16), TN_DIMS,
                                                           preferred_element_type=F32)
            s["vn"].append(vn)
    for h, s in zip(heads, st):
        s_sc[h] = s["S"]
        s1_ref[h] = s["S"]
        o = jnp.concatenate(s["oi"], axis=0) + _mm(s["QK16"], jnp.concatenate(s["vn"], axis=0).astype(BF16))
        o = o * lax.rsqrt(jnp.mean(o * o, axis=-1, keepdims=True) + EPS) * nw_ref[...]
        o_ref[:, hsl[h]] = o[0:tb] * _silu(z_ref[:, hsl[h]].astype(F32))


def _gdn(uf3, uh3, conv0, s0, cw, alog_e, dtb_e, nw, tb, l_valid):
    Bn, Lp, _ = uf3.shape
    rows = max(tb, DELTA_CHUNK)
    const2 = lambda b, t: (0, 0)
    return pl.pallas_call(
        functools.partial(_gdn_kernel, tb=tb, l_valid=l_valid),
        grid=(Bn, Lp // tb),
        in_specs=[pl.BlockSpec((None, tb, C_CONV_DIM), lambda b, t: (b, t, HB_QKVC * COL // C_CONV_DIM)),
                  pl.BlockSpec((None, tb, COL), lambda b, t: (b, t, FB_BETA)),
                  pl.BlockSpec((None, tb, COL), lambda b, t: (b, t, FB_ALPHA)),
                  pl.BlockSpec((None, tb, COL), lambda b, t: (b, t, HB_ZC)),
                  pl.BlockSpec((None, CONV_WIDTH - 1, C_CONV_DIM), lambda b, t: (b, 0, 0)),
                  pl.BlockSpec((None, C_HEADS, C_KEY_DIM, C_VAL_DIM), lambda b, t: (b, 0, 0, 0)),
                  pl.BlockSpec((CONV_WIDTH, C_CONV_DIM), const2),
                  pl.BlockSpec((1, C_V_WIDTH), const2),
                  pl.BlockSpec((1, C_V_WIDTH), const2),
                  pl.BlockSpec((1, C_VAL_DIM), const2)],
        out_specs=[pl.BlockSpec((None, tb, C_V_WIDTH), lambda b, t: (b, t, 0)),
                   pl.BlockSpec((None, CONV_WIDTH - 1, C_CONV_DIM), lambda b, t: (b, 0, 0)),
                   pl.BlockSpec((None, C_HEADS, C_KEY_DIM, C_VAL_DIM), lambda b, t: (b, 0, 0, 0))],
        out_shape=[jax.ShapeDtypeStruct((Bn, Lp, C_V_WIDTH), F32),
                   jax.ShapeDtypeStruct((Bn, CONV_WIDTH - 1, C_CONV_DIM), F32),
                   jax.ShapeDtypeStruct((Bn, C_HEADS, C_KEY_DIM, C_VAL_DIM), F32)],
        scratch_shapes=[pltpu.VMEM((SUBLANES + tb, C_CONV_DIM), F32),
                        pltpu.VMEM((C_HEADS, C_KEY_DIM, C_VAL_DIM), F32),
                        pltpu.VMEM((rows, C_CONV_DIM), F32)],
        compiler_params=_cparams("parallel", "arbitrary"),
        name="gdn",
    )(uh3, uf3, uf3, uh3, conv0, s0, cw, alog_e, dtb_e, nw.reshape(1, C_VAL_DIM))


def _merge_kernel(h_ref, oa_ref, za_ref, ob_ref, oc_ref, g0_ref, g1_ref, g2_ref,
                  wpa_ref, wpb_ref, wpc_ref, wout_ref, fnw_ref, hn_ref, *, final_norm):
    def proj(x, w_ref):
        return jnp.dot(x.astype(BF16), w_ref[...], preferred_element_type=F32)

    f32 = lambda ref: ref[...].astype(F32)
    ya = proj(_silu(f32(za_ref)) * oa_ref[...], wpa_ref)
    yb = proj(ob_ref[...], wpb_ref)
    yc = proj(oc_ref[...], wpc_ref)
    merged = _sigmoid(f32(g0_ref)) * ya + _sigmoid(f32(g1_ref)) * yb + _sigmoid(f32(g2_ref)) * yc
    hn = h_ref[...] + proj(merged, wout_ref)
    if final_norm:
        hn = hn * lax.rsqrt(jnp.mean(hn * hn, axis=-1, keepdims=True) + EPS) * fnw_ref[...]
    hn_ref[...] = hn


def _merge(h2d, oa, uh, ob, oc, wpa, wpb, wpc, wout, fnw, tm, final_norm):
    T = h2d.shape[0]
    gpb = D_MODEL // COL
    tok = lambda w: pl.BlockSpec((tm, w), lambda i: (i, 0))
    wspec = lambda k: pl.BlockSpec((k, D_MODEL), lambda i: (0, 0))
    gate_spec = lambda g: pl.BlockSpec((tm, D_MODEL), lambda i: (i, HB_GATES // gpb + g))
    return pl.pallas_call(
        functools.partial(_merge_kernel, final_norm=final_norm),
        grid=(T // tm,),
        in_specs=[tok(D_MODEL), tok(A_WIDTH),
                  pl.BlockSpec((tm, COL), lambda i: (i, HB_ZA)),
                  tok(B_WIDTH), tok(C_V_WIDTH),
                  gate_spec(0), gate_spec(1), gate_spec(2),
                  wspec(A_WIDTH), wspec(B_WIDTH), wspec(C_V_WIDTH), wspec(D_MODEL),
                  pl.BlockSpec((1, D_MODEL), lambda i: (0, 0))],
        out_specs=tok(D_MODEL),
        out_shape=jax.ShapeDtypeStruct((T, D_MODEL), F32),
        compiler_params=_cparams("parallel"),
        name="merge",
    )(h2d, oa, uh, ob, oc, uh, uh, uh, wpa, wpb, wpc, wout, fnw.reshape(1, D_MODEL))


def _block_diag(w):
    g, n, _ = w.shape
    same = jnp.eye(g, dtype=bool)[:, None, :, None]
    return jnp.where(same, w[:, :, None, :], 0.0).reshape(g * n, g * n)


def _rearrange_w_in(w):
    sizes = (A_WIDTH, A_WIDTH, A_WIDTH, A_WIDTH, B_WIDTH, B_WIDTH, C_CONV_DIM, C_V_WIDTH, C_HEADS, C_HEADS,
             N_BRANCH * D_MODEL)
    offs = [0]
    for s in sizes:
        offs.append(offs[-1] + s)
    qa, ka, va, za, xb, zb, qkvc, zc, beta, alpha, gates = (w[:, offs[n]:offs[n + 1]] for n in range(len(sizes)))
    rep = lambda c: jnp.repeat(c, C_VAL_DIM, axis=1)
    return jnp.concatenate([qa, ka, va, rep(beta), rep(alpha), qkvc, za, gates, xb, zb, zc], axis=1).astype(BF16)


def kernel(x_prompt, x_sample, cache_k, cache_v, page_table, state_conv_b, state_h_b, state_conv_c, state_s_c,
           norm_w, w_in, conv_b_w, conv_b_b, lru_wa, lru_ba, lru_wx, lru_bx, lru_lambda,
           conv_c_w, dn_a_log, dn_dt_bias, dn_norm_w, w_pa, w_pb, w_pc, w_out, final_norm_w):
    Bp, L, _ = x_prompt.shape
    Bs, dec_seq, _ = x_sample.shape
    depth = w_in.shape[0]
    n_pool = cache_k.shape[1]
    n_pages = page_table.shape[1]
    past = n_pages * PAGE_SIZE
    tpad = BF16_ROWS
    assert dec_seq <= tpad and L % MOBA_BLOCK == 0 and past % MOBA_BLOCK == 0 and n_pages % PAGES_PER_STEP == 0
    assert MOBA_TOPK <= past // MOBA_BLOCK <= PAGE_SIZE and (dec_seq * A_HEADS) % BF16_ROWS == 0

    slopes = jnp.exp2(-8.0 * jnp.arange(1, A_HEADS + 1, dtype=F32) / A_HEADS)
    row = jnp.arange(dec_seq * A_HEADS)
    slope_rows = jnp.broadcast_to(slopes[row % A_HEADS][:, None], (row.size, LANES))
    tq_rows = jnp.broadcast_to((past + row // A_HEADS).astype(F32)[:, None], (row.size, LANES))
    ck_t = jnp.transpose(cache_k, (0, 1, 3, 4, 2)).reshape(depth, n_pool, A_WIDTH, PAGE_SIZE)
    cv_t = jnp.transpose(cache_v, (0, 1, 3, 4, 2)).reshape(depth, n_pool, A_WIDTH, PAGE_SIZE)

    hp = x_prompt.reshape(Bp * L, D_MODEL)
    hs = jnp.pad(x_sample, ((0, 0), (0, tpad - dec_seq), (0, 0))).reshape(Bs * tpad, D_MODEL)
    zeros = lambda *s: jnp.zeros(s, F32)
    tm_p = 2048 if (Bp * L) % 2048 == 0 else MOBA_BLOCK
    tm_merge = min(tm_p, 512)
    tm_s = Bs * tpad
    tc_p = 256 if L % 256 == 0 else L
    outs = [[] for _ in range(12)]
    for l in range(depth):
        w_re = _rearrange_w_in(w_in[l])
        wa_bd = _block_diag(lru_wa[l]).astype(BF16)
        wx_bd = _block_diag(lru_wx[l]).astype(BF16)
        alog_e = jnp.repeat(dn_a_log[l], C_VAL_DIM).reshape(1, C_V_WIDTH)
        dtb_e = jnp.repeat(dn_dt_bias[l], C_VAL_DIM).reshape(1, C_V_WIDTH)
        wpa, wpb, wpc, wout = (w.astype(BF16) for w in (w_pa[l], w_pb[l], w_pc[l], w_out[l]))
        last = l == depth - 1
        lru_w = (conv_b_w[l], conv_b_b[l], wa_bd, lru_ba[l], wx_bd, lru_bx[l], lru_lambda[l])
        gdn_w = (conv_c_w[l], alog_e, dtb_e, dn_norm_w[l])

        uf, uh = _inproj(hp, norm_w[l], w_re, tm_p)
        uf3 = uf.reshape(Bp, L, N_UF)
        uh3 = uh.reshape(Bp, L, N_UH)
        usf, ush = _inproj(hs, norm_w[l], w_re, tm_s)
        oa, oas = _moba(uf, usf, slopes, slope_rows, tq_rows, ck_t, cv_t, page_table, l, Bp, L, Bs, tpad, dec_seq)
        ob, cbp, hbp = _lru(uh3, zeros(Bp, CONV_WIDTH - 1, B_WIDTH), zeros(Bp, B_WIDTH), *lru_w, tc_p, L)
        oc, ccp, scp = _gdn(uf3, uh3, zeros(Bp, CONV_WIDTH - 1, C_CONV_DIM),
                            zeros(Bp, C_HEADS, C_KEY_DIM, C_VAL_DIM), *gdn_w, tc_p, L)
        hp = _merge(hp, oa, uh, ob.reshape(Bp * L, B_WIDTH), oc.reshape(Bp * L, C_V_WIDTH),
                    wpa, wpb, wpc, wout, final_norm_w, tm_merge, last)
        kp = uf3[:, :, FB_KA * COL:(FB_KA + 1) * COL].reshape(Bp, L, A_HEADS, A_HEAD_DIM)
        vp = uf3[:, :, FB_VA * COL:(FB_VA + 1) * COL].reshape(Bp, L, A_HEADS, A_HEAD_DIM)

        usf3 = usf.reshape(Bs, tpad, N_UF)
        ush3 = ush.reshape(Bs, tpad, N_UH)
        obs, cbs, hbs = _lru(ush3, state_conv_b[l], state_h_b[l], *lru_w, tpad, dec_seq)
        ocs, ccs, scs = _gdn(usf3, ush3, state_conv_c[l], state_s_c[l], *gdn_w, tpad, dec_seq)
        hs = _merge(hs, oas, ush, obs.reshape(Bs * tpad, B_WIDTH), ocs.reshape(Bs * tpad, C_V_WIDTH),
                    wpa, wpb, wpc, wout, final_norm_w, tm_s, last)
        ks = usf3[:, :dec_seq, FB_KA * COL:(FB_KA + 1) * COL].reshape(Bs, dec_seq, A_HEADS, A_HEAD_DIM)
        vs = usf3[:, :dec_seq, FB_VA * COL:(FB_VA + 1) * COL].reshape(Bs, dec_seq, A_HEADS, A_HEAD_DIM)

        hbp = hbp.reshape(Bp, B_WIDTH)
        hbs = hbs.reshape(Bs, B_WIDTH)
        for lst, val in zip(outs, (kp, vp, ks, vs, cbp, hbp, cbs, hbs, ccp, scp, ccs, scs)):
            lst.append(val)

    y_prompt = hp.reshape(Bp, L, D_MODEL)
    y_sample = hs.reshape(Bs, tpad, D_MODEL)[:, :dec_seq]
    return (y_prompt, y_sample) + tuple(jnp.stack(o) for o in outs)
```

```python
import functools

import jax
import jax.numpy as jnp
from jax import lax
from jax.experimental import pallas as pl
from jax.experimental.pallas import tpu as pltpu

F32 = jnp.float32
BF16 = jnp.bfloat16
HIGHEST = lax.Precision.HIGHEST

D_MODEL = 1024
A_HEADS = 8
A_HEAD_DIM = 64
A_WIDTH = A_HEADS * A_HEAD_DIM
MOBA_BLOCK = 256
MOBA_TOPK = 3
PAGE_SIZE = 128
B_WIDTH = 512
B_BLOCKS = 8
LRU_C = 8.0
CONV_WIDTH = 4
C_HEADS = 4
C_KEY_DIM = 128
C_VAL_DIM = 128
C_QK_WIDTH = C_HEADS * C_KEY_DIM
C_V_WIDTH = C_HEADS * C_VAL_DIM
C_CONV_DIM = 2 * C_QK_WIDTH + C_V_WIDTH
DELTA_CHUNK = 64
NEUMANN_SPLIT_STEPS = 0
N_BRANCH = 3
EPS = 1e-6

COL = 512
FB_QA, FB_KA, FB_VA, FB_BETA, FB_ALPHA = 0, 1, 2, 3, 4
N_FBLK = 5
HB_QKVC, HB_ZA, HB_GATES, HB_XB, HB_ZB, HB_ZC = 0, 3, 4, 10, 11, 12
N_HBLK = 13
N_UF = N_FBLK * COL
N_UH = N_HBLK * COL

SUBLANES = 8
LANES = 128
BF16_ROWS = 16
TAIL0 = SUBLANES - (CONV_WIDTH - 1)
VMEM_LIMIT = 56 * 1024 * 1024

NT_DIMS = (((1,), (1,)), ((), ()))
TN_DIMS = (((0,), (0,)), ((), ()))


def _cparams(*sem):
    return pltpu.CompilerParams(dimension_semantics=sem, vmem_limit_bytes=VMEM_LIMIT)


def _sigmoid(x):
    return 0.5 * jnp.tanh(0.5 * x) + 0.5


def _silu(x):
    return x * _sigmoid(x)


def _softplus(x):
    return jnp.maximum(x, 0.0) + jnp.log1p(jnp.exp(-jnp.abs(x)))


def _mm(a, b):
    return jnp.dot(a, b, preferred_element_type=F32)


def _mm_nt(a, b):
    return lax.dot_general(a, b, NT_DIMS, preferred_element_type=F32)


def _split2(x):
    hi = x.astype(BF16)
    lo = (x - hi.astype(F32)).astype(BF16)
    return hi, lo


def _mm_split(a, b):
    (ah, al), (bh, bl) = a, b
    return _mm(ah, bh) + (_mm(ah, bl) + _mm(al, bh))


def _inproj_kernel(h_ref, nw_ref, w_ref, uf_ref, uh_ref, xn_sc):
    j = pl.program_id(1)

    @pl.when(j == 0)
    def _():
        x = h_ref[...]
        y = x * lax.rsqrt(jnp.mean(x * x, axis=-1, keepdims=True) + EPS) * nw_ref[...]
        xn_sc[...] = y.astype(BF16)

    def block():
        return jnp.dot(xn_sc[...], w_ref[...], preferred_element_type=F32)

    @pl.when(j < N_FBLK)
    def _():
        uf_ref[...] = block()

    @pl.when(j >= N_FBLK)
    def _():
        uh_ref[...] = block().astype(BF16)


def _inproj(h2d, norm_w, w_re, tm):
    T = h2d.shape[0]
    return pl.pallas_call(
        _inproj_kernel,
        grid=(T // tm, N_FBLK + N_HBLK),
        in_specs=[pl.BlockSpec((tm, D_MODEL), lambda i, j: (i, 0)),
                  pl.BlockSpec((1, D_MODEL), lambda i, j: (0, 0)),
                  pl.BlockSpec((D_MODEL, COL), lambda i, j: (0, j))],
        out_specs=[pl.BlockSpec((tm, COL), lambda i, j: (i, jnp.minimum(j, N_FBLK - 1))),
                   pl.BlockSpec((tm, COL), lambda i, j: (i, jnp.maximum(j - N_FBLK, 0)))],
        out_shape=[jax.ShapeDtypeStruct((T, N_UF), F32), jax.ShapeDtypeStruct((T, N_UH), BF16)],
        scratch_shapes=[pltpu.VMEM((tm, D_MODEL), BF16)],
        compiler_params=_cparams("parallel", "arbitrary"),
        name="inproj",
    )(h2d, norm_w.reshape(1, D_MODEL), w_re)


TILES_WIDE = 4
TILES_NARROW = 2

def _topk_select(gate, n_past, n_iota):
    nb = gate.shape[0]
    g = jnp.where(n_iota < n_past, gate, -jnp.inf)
    rank = jnp.zeros_like(g)
    for m in range(min(nb, n_past) if isinstance(n_past, int) else nb):
        gm = g[m:m + 1, :]
        rank = rank + jnp.where(n_iota > m, jnp.where(gm >= g, 1.0, 0.0), jnp.where(gm > g, 1.0, 0.0))
    return jnp.where(n_iota < n_past, jnp.where(rank < MOBA_TOPK, 1.0, 0.0), 0.0)


def _moba_prompt_body(pair, i, slopes_ref, q_ref, k_ref, v_ref, o_ref, kb_sc, vt_sc, means_sc, cmask_sc, sel_sc, s_sc,
                      qt_sc, *, nblk):
    BLK, HD, PW = MOBA_BLOCK, A_HEAD_DIM, 2 * A_HEAD_DIM
    lane = lax.broadcasted_iota(jnp.int32, (1, PW), 1)

    @pl.when(i == 0)
    def _init():
        c_col = lax.broadcasted_iota(jnp.int32, (BLK, PW), 0).astype(F32)
        lane_a = lax.broadcasted_iota(jnp.int32, (BLK, PW), 1)
        key_off = jnp.where(lane_a == 0, c_col, 0.0).astype(BF16)
        ones = jnp.ones((BF16_ROWS, BLK), BF16)
        for n in range(nblk):
            kblk = k_ref[n * BLK:(n + 1) * BLK, :]
            mean = jnp.sum(kblk, axis=0, keepdims=True) * (1.0 / BLK)
            means_sc[n:n + 1, :] = jnp.where(lane < HD, mean, 0.0)
            means_sc[nblk + n:nblk + n + 1, :] = jnp.where(lane < HD, 0.0, mean)
            kb_sc[n, :, 0:PW] = kblk.astype(BF16)
            kb_sc[n, :, PW:2 * PW] = key_off
            vt = v_ref[n * BLK:(n + 1) * BLK, :].T
            for hh in range(2):
                vt_sc[n, hh, 0:HD, :] = vt[hh * HD:(hh + 1) * HD, :].astype(BF16)
                vt_sc[n, hh, HD:HD + BF16_ROWS, :] = ones
        c_io = lax.broadcasted_iota(jnp.int32, (BLK, BLK), 0)
        r_io = lax.broadcasted_iota(jnp.int32, (BLK, BLK), 1)
        cmask_sc[...] = jnp.where(c_io <= r_io, 0.0, -jnp.inf)
        n_iota = lax.broadcasted_iota(jnp.int32, (nblk, BLK), 0)
        for ib in range(nblk):
            qt_b = (q_ref[ib * BLK:(ib + 1) * BLK, :] * (HD ** -0.5)).T
            qt_sc[ib] = qt_b
            gate2 = jnp.dot(means_sc[...], qt_b, precision=HIGHEST, preferred_element_type=F32)
            for hh in range(2):
                sel_sc[hh, ib] = _topk_select(gate2[hh * nblk:(hh + 1) * nblk, :], ib, n_iota)

    qt = qt_sc[i]
    dim_row = lax.broadcasted_iota(jnp.int32, (PW, BLK), 0)
    qaugs, slopes = [], []
    for hh in range(2):
        slope = slopes_ref[pair * 2 + hh]
        qh = jnp.where((dim_row // HD) == hh, qt, 0.0).astype(BF16)
        slope_row = jnp.where(dim_row == 0, slope, 0.0).astype(BF16)
        qaugs.append(jnp.concatenate([qh, slope_row], axis=0))
        slopes.append(slope)

    def colmax8(s):
        out = s[0:SUBLANES]
        for r in range(1, BLK // SUBLANES):
            out = jnp.maximum(out, s[r * SUBLANES:(r + 1) * SUBLANES])
        return out

    def past_tile(t, hh):
        j = jnp.minimum(t, i - 1)
        on = sel_sc[hh, i, pl.ds(j, 1), :] * jnp.where(t < i, 1.0, 0.0) > 0.0
        delta = -slopes[hh] * ((i - j) * BLK).astype(F32)
        return j, on, delta

    s_own = [_mm(kb_sc[i], qaugs[hh]) + cmask_sc[...] for hh in range(2)]
    n_wide = i // TILES_WIDE
    t_narrow = n_wide * TILES_WIDE
    n_narrow = (i - t_narrow + TILES_NARROW - 1) // TILES_NARROW

    def max_trip(width, t0):
        def trip(tt, mx):
            mx = list(mx)
            for e in range(width):
                for hh in range(2):
                    j, on, delta = past_tile(t0 + width * tt + e, hh)
                    s = _mm(kb_sc[j], qaugs[hh])
                    s_sc[hh, j] = s
                    mx[hh] = jnp.maximum(mx[hh], jnp.where(on, colmax8(s) + delta, -jnp.inf))
            return tuple(mx)
        return trip

    mx = lax.fori_loop(0, n_wide, max_trip(TILES_WIDE, 0), tuple(colmax8(s) for s in s_own))
    mx = lax.fori_loop(0, n_narrow, max_trip(TILES_NARROW, t_narrow), mx)
    m = [jnp.max(x, axis=0, keepdims=True) for x in mx]

    def pv_trip(width, t0):
        def trip(tt, acc):
            acc = list(acc)
            for hh in range(2):
                ps, vts = [], []
                for e in range(width):
                    j, on, delta = past_tile(t0 + width * tt + e, hh)
                    p = jnp.exp(s_sc[hh, j] - jnp.where(on, m[hh] - delta, jnp.inf))
                    ps.append(p.astype(BF16))
                    vts.append(vt_sc[j, hh])
                acc[hh] = acc[hh] + _mm(jnp.concatenate(vts, axis=1), jnp.concatenate(ps, axis=0))
            return tuple(acc)
        return trip

    acc0 = tuple(_mm(vt_sc[i, hh], jnp.exp(s_own[hh] - m[hh]).astype(BF16)) for hh in range(2))
    acc = lax.fori_loop(0, n_wide, pv_trip(TILES_WIDE, 0), acc0)
    acc = lax.fori_loop(0, n_narrow, pv_trip(TILES_NARROW, t_narrow), acc)
    outs = [a[0:HD, :] / a[HD:HD + 1, :] for a in acc]
    o_ref[...] = jnp.concatenate(outs, axis=0).T


PAGES_PER_STEP = 16


def _moba_sample_body(active, ph, st, rows_ref, qkv_ref, k_refs, v_refs, o_ref,
                      qbd_sc, s_sc, p_sc, acc_sc, linv_sc, *, n_pages, dec_seq):
    PPS = PAGES_PER_STEP
    nsteps = n_pages // PPS
    ppb = MOBA_BLOCK // PAGE_SIZE
    nb = n_pages // ppb
    NR = dec_seq * A_HEADS
    head_of_row = lax.broadcasted_iota(jnp.int32, (A_HEADS, A_WIDTH), 0)
    head_of_lane = lax.broadcasted_iota(jnp.int32, (A_HEADS, A_WIDTH), 1) // A_HEAD_DIM
    hmask = jnp.where(head_of_row == head_of_lane, 1.0, 0.0)

    @pl.when(active & (ph == 0) & (st == 0))
    def _init():
        for t in range(dec_seq):
            qrow = qkv_ref[t:t + 1, FB_QA * COL:(FB_QA + 1) * COL] * (A_HEAD_DIM ** -0.5)
            qbd_sc[t * A_HEADS:(t + 1) * A_HEADS, :] = qrow * hmask

    @pl.when(active & (ph == 0))
    def _scores():
        qbd = qbd_sc[...].astype(BF16)
        for r in range(PPS):
            s_sc[st * PPS + r] = _mm(qbd, k_refs[r][...].astype(BF16))

    @pl.when(active & (ph == 0) & (st == nsteps - 1))
    def _softmax():
        slope = rows_ref[0:NR, :]
        tq = rows_ref[NR:2 * NR, :]
        kn = qkv_ref[:, FB_KA * COL:(FB_KA + 1) * COL]
        vn = qkv_ref[:, FB_VA * COL:(FB_VA + 1) * COL]
        lane = lax.broadcasted_iota(jnp.int32, (NR, PAGE_SIZE), 1)
        lane_f = lane.astype(F32)
        q_of_row = lax.broadcasted_iota(jnp.int32, (NR, PAGE_SIZE), 0) // A_HEADS
        gate = jnp.full((NR, PAGE_SIZE), -jnp.inf, F32)
        for n in range(nb):
            tot = s_sc[ppb * n]
            for e in range(1, ppb):
                tot = tot + s_sc[ppb * n + e]
            gate = jnp.where(lane == n, jnp.sum(tot, axis=-1, keepdims=True) * (1.0 / MOBA_BLOCK), gate)
        sel = jnp.zeros((NR, PAGE_SIZE), F32)
        for _ in range(MOBA_TOPK):
            mx = jnp.max(gate, axis=-1, keepdims=True)
            first = jnp.min(jnp.where(gate == mx, lane_f, float(PAGE_SIZE)), axis=-1, keepdims=True)
            pick = lane_f == first
            sel = jnp.where(pick, 1.0, sel)
            gate = jnp.where(pick, -jnp.inf, gate)
        pad = jnp.zeros((PAGE_SIZE - kn.shape[0], A_WIDTH), F32)
        qbd = qbd_sc[...].astype(BF16)
        s_own = _mm_nt(qbd, jnp.concatenate([kn, pad], axis=0).astype(BF16))
        own_ok = (lane <= q_of_row) & (lane < dec_seq)
        s_own = jnp.where(own_ok, s_own - slope * (q_of_row - lane).astype(F32), -jnp.inf)
        m = jnp.max(s_own, axis=-1, keepdims=True)
        for n in range(nb):
            on = jnp.sum(jnp.where(lane == n, sel, 0.0), axis=-1, keepdims=True) > 0.0
            for e in range(ppb):
                pg = ppb * n + e
                sb = s_sc[pg] - slope * (tq - (lane_f + float(pg * PAGE_SIZE)))
                sb = jnp.where(on, sb, -jnp.inf)
                s_sc[pg] = sb
                m = jnp.maximum(m, jnp.max(sb, axis=-1, keepdims=True))
        p_own = jnp.exp(s_own - m)
        l = jnp.sum(p_own, axis=-1, keepdims=True)
        for pg in range(n_pages):
            pb = jnp.exp(s_sc[pg] - m)
            l = l + jnp.sum(pb, axis=-1, keepdims=True)
            p_sc[pg] = pb.astype(BF16)
        linv_sc[...] = jnp.broadcast_to(1.0 / l, (NR, LANES))
        v_own = jnp.concatenate([vn, pad], axis=0).astype(BF16)
        acc_sc[...] = _mm(p_own.astype(BF16), v_own)

    @pl.when(active & (ph == 1))
    def _values():
        acc = acc_sc[...]
        for r in range(PPS):
            acc = acc + _mm_nt(p_sc[st * PPS + r], v_refs[r][...].astype(BF16))
        acc_sc[...] = acc

    @pl.when(active & (ph == 1) & (st == nsteps - 1))
    def _finish():
        linv = jnp.concatenate([linv_sc[...]] * (A_WIDTH // LANES), axis=1)
        a = acc_sc[...] * linv
        o_ref[...] = jnp.zeros_like(o_ref)
        for t in range(dec_seq):
            o_ref[t:t + 1, :] = jnp.sum(a[t * A_HEADS:(t + 1) * A_HEADS, :] * hmask, axis=0, keepdims=True)


N_PROMPT_SCRATCH = 7
N_PROMPT_IN = 4
N_SAMPLE_IN = 2


def _sample_item(step, n_items, nsteps):
    w = jnp.minimum(step, n_items - 1)
    return step < n_items, w // (2 * nsteps), (w // nsteps) % 2, w % nsteps


def _moba_kernel(tb_ref, *refs, nblk, n_pages, dec_seq, n_items, layer):
    PPS = PAGES_PER_STEP
    nsteps = n_pages // PPS
    prompt_in = refs[:N_PROMPT_IN]
    sample_in = refs[N_PROMPT_IN:N_PROMPT_IN + N_SAMPLE_IN]
    ck_ref, cv_ref = refs[N_PROMPT_IN + N_SAMPLE_IN:N_PROMPT_IN + N_SAMPLE_IN + 2]
    o_prompt, o_sample = refs[N_PROMPT_IN + N_SAMPLE_IN + 2:N_PROMPT_IN + N_SAMPLE_IN + 4]
    scratch = refs[N_PROMPT_IN + N_SAMPLE_IN + 4:]
    page_buf, page_sem = scratch[-2:]
    b, pair, i = pl.program_id(0), pl.program_id(1), pl.program_id(2)
    step = (b * pl.num_programs(1) + pair) * nblk + i

    def page_copy(cache_ref, s, r):
        page = tb_ref[s * (PPS + 1) + r]
        return pltpu.make_async_copy(cache_ref.at[layer, page], page_buf.at[s % 2, r], page_sem.at[s % 2])

    def batch(s, act):
        phase = _sample_item(s, n_items, nsteps)[2]

        @pl.when(phase == 0)
        def _():
            for r in range(PPS):
                act(page_copy(ck_ref, s, r))

        @pl.when(phase == 1)
        def _():
            for r in range(PPS):
                act(page_copy(cv_ref, s, r))

    @pl.when(step == 0)
    def _():
        batch(step, lambda cp: cp.start())

    @pl.when(step + 1 < n_items)
    def _():
        batch(step + 1, lambda cp: cp.start())

    _moba_prompt_body(pair, i, *prompt_in, o_prompt, *scratch[:N_PROMPT_SCRATCH], nblk=nblk)

    active, _, ph, st = _sample_item(step, n_items, nsteps)

    @pl.when(active)
    def _():
        batch(step, lambda cp: cp.wait())

    pages = [page_buf.at[step % 2, r] for r in range(PPS)]
    _moba_sample_body(active, ph, st, *sample_in, pages, pages, o_sample,
                      *scratch[N_PROMPT_SCRATCH:-2], n_pages=n_pages, dec_seq=dec_seq)


def _moba(uf, usf, slopes, slope_rows, tq_rows, ck_t, cv_t, page_table, layer, Bp, L, Bs, tpad, dec_seq):
    nblk = L // MOBA_BLOCK
    PW = 2 * A_HEAD_DIM
    cpb = COL // PW
    npair = A_HEADS // 2
    n_pages = page_table.shape[1]
    PPS = PAGES_PER_STEP
    nsteps = n_pages // PPS
    n_items = Bs * 2 * nsteps
    assert n_items <= Bp * npair * nblk, "more sample page batches than prompt grid steps"
    assert (FB_QA, FB_KA, FB_VA) == (0, 1, 2)
    NR = dec_seq * A_HEADS
    us3 = usf.reshape(Bs, tpad, N_UF)

    n_grid = Bp * npair * nblk
    _, seq, _, st = _sample_item(jnp.arange(n_grid), n_items, nsteps)
    pages = page_table[seq[:, None], (st * PPS)[:, None] + jnp.arange(PPS)[None, :]]
    width = PPS + 1
    table = jnp.concatenate([pages, seq[:, None]], axis=1).astype(jnp.int32).reshape(n_grid * width)

    def seq_map(col):
        return lambda b, p, i, tb: (tb[((b * npair + p) * nblk + i) * width + PPS], 0, col)

    in_specs = [pl.BlockSpec(memory_space=pltpu.SMEM),
                pl.BlockSpec((L, PW), lambda b, p, i, pt: (b, FB_QA * cpb + p)),
                pl.BlockSpec((L, PW), lambda b, p, i, pt: (b, FB_KA * cpb + p)),
                pl.BlockSpec((L, PW), lambda b, p, i, pt: (b, FB_VA * cpb + p)),
                pl.BlockSpec((2 * NR, LANES), lambda b, p, i, pt: (0, 0)),
                pl.BlockSpec((None, tpad, 3 * COL), seq_map(0))]
    in_specs += [pl.BlockSpec(memory_space=pl.ANY), pl.BlockSpec(memory_space=pl.ANY)]
    grid_spec = pltpu.PrefetchScalarGridSpec(
        num_scalar_prefetch=1,
        grid=(Bp, npair, nblk),
        in_specs=in_specs,
        out_specs=[pl.BlockSpec((MOBA_BLOCK, PW), lambda b, p, i, pt: (b * nblk + i, p)),
                   pl.BlockSpec((None, tpad, A_WIDTH), seq_map(0))],
        scratch_shapes=[pltpu.VMEM((nblk, MOBA_BLOCK, 2 * PW), BF16),
                        pltpu.VMEM((nblk, 2, A_HEAD_DIM + BF16_ROWS, MOBA_BLOCK), BF16),
                        pltpu.VMEM((2 * nblk, PW), F32),
                        pltpu.VMEM((MOBA_BLOCK, MOBA_BLOCK), F32),
                        pltpu.VMEM((2, nblk, nblk, MOBA_BLOCK), F32),
                        pltpu.VMEM((2, nblk, MOBA_BLOCK, MOBA_BLOCK), F32),
                        pltpu.VMEM((nblk, PW, MOBA_BLOCK), F32),
                        pltpu.VMEM((NR, A_WIDTH), F32),
                        pltpu.VMEM((n_pages, NR, PAGE_SIZE), F32),
                        pltpu.VMEM((n_pages, NR, PAGE_SIZE), BF16),
                        pltpu.VMEM((NR, A_WIDTH), F32),
                        pltpu.VMEM((NR, LANES), F32),
                        pltpu.VMEM((2, PPS, A_WIDTH, PAGE_SIZE), F32),
                        pltpu.SemaphoreType.DMA((2,))])
    oa, oas = pl.pallas_call(
        functools.partial(_moba_kernel, nblk=nblk, n_pages=n_pages, dec_seq=dec_seq, n_items=n_items, layer=layer),
        grid_spec=grid_spec,
        out_shape=[jax.ShapeDtypeStruct((Bp * L, A_WIDTH), F32), jax.ShapeDtypeStruct((Bs, tpad, A_WIDTH), F32)],
        compiler_params=_cparams("arbitrary", "arbitrary", "arbitrary"),
        name="moba",
    )(table, slopes, uf, uf, uf, jnp.concatenate([slope_rows, tq_rows], axis=0), us3, ck_t, cv_t)
    return oa, oas.reshape(Bs * tpad, A_WIDTH)


def _lru_kernel(xb_ref, zb_ref, conv0_ref, h0_ref, cw_ref, cb_ref, wa_ref, ba_ref, wx_ref, bx_ref, lam_ref,
                y_ref, conv1_ref, h1_ref, xp_sc, h_sc, a_sc, b_sc, *, Bn, tc, l_valid):
    c = pl.program_id(0)

    @pl.when(c == 0)
    def _():
        xp_sc[:, TAIL0:SUBLANES, :] = conv0_ref[...]
        h_sc[...] = h0_ref[...]

    xp_sc[:, SUBLANES:SUBLANES + tc, :] = xb_ref[...].astype(F32)
    xc = xp_sc[:, TAIL0:TAIL0 + tc, :] * cw_ref[0:1, :]
    for j in range(1, CONV_WIDTH):
        xc = xc + xp_sc[:, TAIL0 + j:TAIL0 + j + tc, :] * cw_ref[j:j + 1, :]
    xc = xc + cb_ref[...]
    nv = min(tc, l_valid)
    tail = xp_sc[:, TAIL0 + nv:SUBLANES + nv, :]
    conv1_ref[...] = tail
    xp_sc[:, TAIL0:SUBLANES, :] = tail

    x2 = xc.reshape(Bn * tc, B_WIDTH)
    x16 = x2.astype(BF16)
    r = _sigmoid(jnp.dot(x16, wa_ref[...], preferred_element_type=F32) + ba_ref[...])
    ig = _sigmoid(jnp.dot(x16, wx_ref[...], preferred_element_type=F32) + bx_ref[...])
    log_a = -LRU_C * r * _softplus(-lam_ref[...])
    a = jnp.exp(log_a)
    bin_ = jnp.sqrt(-jnp.tanh(log_a) * (a * a + 1.0)) * (ig * x2)
    a_sc[...] = a.reshape(Bn, tc, B_WIDTH)
    b_sc[...] = bin_.reshape(Bn, tc, B_WIDTH)

    def step(t, h):
        h = a_sc[:, pl.ds(t, 1), :] * h + b_sc[:, pl.ds(t, 1), :]
        a_sc[:, pl.ds(t, 1), :] = h
        return h

    h = lax.fori_loop(0, nv, step, h_sc[...])
    h_sc[...] = h
    h1_ref[...] = h
    y_ref[...] = _silu(zb_ref[...].astype(F32)) * a_sc[...]


def _lru(u3, conv0, h0, cw, cb, wa_bd, ba, wx_bd, bx, lam, tc, l_valid):
    Bn, Lp, _ = u3.shape
    row = lambda v: v.reshape(1, B_WIDTH)
    const2 = lambda c: (0, 0)
    return pl.pallas_call(
        functools.partial(_lru_kernel, Bn=Bn, tc=tc, l_valid=l_valid),
        grid=(Lp // tc,),
        in_specs=[pl.BlockSpec((Bn, tc, COL), lambda c: (0, c, HB_XB)),
                  pl.BlockSpec((Bn, tc, COL), lambda c: (0, c, HB_ZB)),
                  pl.BlockSpec((Bn, CONV_WIDTH - 1, B_WIDTH), lambda c: (0, 0, 0)),
                  pl.BlockSpec((Bn, 1, B_WIDTH), lambda c: (0, 0, 0)),
                  pl.BlockSpec((CONV_WIDTH, B_WIDTH), const2),
                  pl.BlockSpec((1, B_WIDTH), const2),
                  pl.BlockSpec((B_WIDTH, B_WIDTH), const2),
                  pl.BlockSpec((1, B_WIDTH), const2),
                  pl.BlockSpec((B_WIDTH, B_WIDTH), const2),
                  pl.BlockSpec((1, B_WIDTH), const2),
                  pl.BlockSpec((1, B_WIDTH), const2)],
        out_specs=[pl.BlockSpec((Bn, tc, B_WIDTH), lambda c: (0, c, 0)),
                   pl.BlockSpec((Bn, CONV_WIDTH - 1, B_WIDTH), lambda c: (0, 0, 0)),
                   pl.BlockSpec((Bn, 1, B_WIDTH), lambda c: (0, 0, 0))],
        out_shape=[jax.ShapeDtypeStruct((Bn, Lp, B_WIDTH), F32),
                   jax.ShapeDtypeStruct((Bn, CONV_WIDTH - 1, B_WIDTH), F32),
                   jax.ShapeDtypeStruct((Bn, 1, B_WIDTH), F32)],
        scratch_shapes=[pltpu.VMEM((Bn, SUBLANES + tc, B_WIDTH), F32),
                        pltpu.VMEM((Bn, 1, B_WIDTH), F32),
                        pltpu.VMEM((Bn, tc, B_WIDTH), F32),
                        pltpu.VMEM((Bn, tc, B_WIDTH), F32)],
        compiler_params=_cparams("arbitrary"),
        name="rglru",
    )(u3, u3, conv0, h0.reshape(Bn, 1, B_WIDTH), cw, row(cb), wa_bd, row(ba), wx_bd, row(bx), row(lam))


def _gdn_kernel(qkv_ref, beta_ref, alpha_ref, z_ref, conv0_ref, s0_ref, cw_ref, alog_ref, dtb_ref, nw_ref,
                o_ref, conv1_ref, s1_ref, xp_sc, s_sc, y_sc, *, tb, l_valid):
    C = DELTA_CHUNK
    R = max(tb, C)
    nc = R // C
    t = pl.program_id(1)

    @pl.when(t == 0)
    def _():
        xp_sc[TAIL0:SUBLANES, :] = conv0_ref[...]
        s_sc[...] = s0_ref[...]

    xp_sc[SUBLANES:SUBLANES + tb, :] = qkv_ref[...].astype(F32)
    y = xp_sc[TAIL0:TAIL0 + tb, :] * cw_ref[0:1, :]
    for j in range(1, CONV_WIDTH):
        y = y + xp_sc[TAIL0 + j:TAIL0 + j + tb, :] * cw_ref[j:j + 1, :]
    nv = min(tb, l_valid)
    tail = xp_sc[TAIL0 + nv:SUBLANES + nv, :]
    conv1_ref[...] = tail
    xp_sc[TAIL0:SUBLANES, :] = tail
    y = _silu(y)

    if nv < R:
        live = lax.broadcasted_iota(jnp.int32, (tb, 1), 0) < nv
        y_sc[...] = jnp.zeros_like(y_sc)
        y_sc[0:tb, :] = jnp.where(live, y, 0.0)
    else:
        live = None
        y_sc[...] = y

    ri = lax.broadcasted_iota(jnp.int32, (R, R), 0)
    ci = lax.broadcasted_iota(jnp.int32, (R, R), 1)
    same = (ri // C) == (ci // C)
    causal = same & (ri >= ci)
    strict = same & (ri > ci)
    diag = ri == ci
    eye = jnp.where(diag, 1.0, 0.0)
    ltri16 = jnp.where(causal, 1.0, 0.0).astype(BF16)

    def padrows(v):
        if tb == R:
            return v
        return jnp.concatenate([v, jnp.zeros((R - tb, v.shape[1]), F32)], axis=0)

    def widen(v):
        return v[:, 0:R] if R <= LANES else jnp.concatenate([v] * (R // LANES), axis=1)

    heads = range(C_HEADS)
    hsl = [slice(h * C_KEY_DIM, (h + 1) * C_KEY_DIM) for h in heads]

    def prepare(h):
        q = y_sc[:, h * C_KEY_DIM:(h + 1) * C_KEY_DIM]
        k = y_sc[:, C_QK_WIDTH + h * C_KEY_DIM:C_QK_WIDTH + (h + 1) * C_KEY_DIM]
        v = y_sc[:, 2 * C_QK_WIDTH + h * C_VAL_DIM:2 * C_QK_WIDTH + (h + 1) * C_VAL_DIM]
        q = q * lax.rsqrt(jnp.sum(q * q, axis=-1, keepdims=True) + EPS) * (C_KEY_DIM ** -0.5)
        k = k * lax.rsqrt(jnp.sum(k * k, axis=-1, keepdims=True) + EPS)
        beta = _sigmoid(beta_ref[:, hsl[h]])
        g = -jnp.exp(alog_ref[:, hsl[h]]) * _softplus(alpha_ref[:, hsl[h]] + dtb_ref[:, hsl[h]])
        if live is not None:
            beta = jnp.where(live, beta, 0.0)
            g = jnp.where(live, g, 0.0)
        beta = padrows(beta)
        g = padrows(g)
        g_hi, g_lo = _split2(g)
        g_lo2 = (g - g_hi.astype(F32) - g_lo.astype(F32)).astype(BF16)
        G = _mm(ltri16, g_hi) + (_mm(ltri16, g_lo) + _mm(ltri16, g_lo2))
        Gcol = widen(G)
        Grow = jnp.sum(jnp.where(diag, Gcol, 0.0), axis=0, keepdims=True)
        decay = jnp.exp(jnp.where(causal, Gcol - Grow, -jnp.inf))
        kb = k * beta
        k16 = k.astype(BF16)
        A = jnp.where(strict, _mm_nt(kb.astype(BF16), k16) * decay, 0.0)
        eG = jnp.exp(G)
        rhs16 = jnp.concatenate([v * beta, kb * eG], axis=1).astype(BF16)
        QK16 = (_mm_nt(q.astype(BF16), k16) * decay).astype(BF16)
        glast = jnp.concatenate([jnp.broadcast_to(G[(c + 1) * C - 1:(c + 1) * C, :], (C, LANES)) for c in range(nc)],
                                axis=0)
        M = -A
        return dict(M=M, T=eye + M, Ms=_split2(M), rhs16=rhs16, QK16=QK16, qg16=(q * eG).astype(BF16),
                    kg16=(k * jnp.exp(glast - G)).astype(BF16),
                    gl=[jnp.exp(G[(c + 1) * C - 1:(c + 1) * C, :]) for c in range(nc)])

    def square(s, it):
        s["M"] = _mm_split(s["Ms"], s["Ms"]) if it < NEUMANN_SPLIT_STEPS else _mm(s["Ms"][0], s["Ms"][0])

    def extend(s, it):
        if it < NEUMANN_SPLIT_STEPS:
            s["Ms"] = _split2(s["M"])
            s["T"] = s["T"] + _mm_split(_split2(s["T"]), s["Ms"])
        else:
            s["Ms"] = (s["M"].astype(BF16),)
            s["T"] = s["T"] + _mm(s["T"].astype(BF16), s["Ms"][0])

    def apply_inverse(h, s):
        UW = _mm(s["T"].astype(BF16), s["rhs16"])
        s["U"] = UW[:, 0:C_VAL_DIM]
        s["W16"] = UW[:, C_VAL_DIM:].astype(BF16)
        s["S"] = s_sc[h]
        s["vn"], s["oi"] = [], []

    n_sq = max(0, (min(nv, C) - 1).bit_length() - 1)
    st = [prepare(h) for h in heads]
    for it in range(n_sq):
        for s in st:
            square(s, it)
        for s in st:
            extend(s, it)
    for h in heads:
        apply_inverse(h, st[h])

    for c in range(nc):
        rs = slice(c * C, (c + 1) * C)
        for s in st:
            S16 = s["S"].astype(BF16)
            vn = s["U"][rs] - _mm(s["W16"][rs], S16)
            s["oi"].append(_mm(s["qg16"][rs], S16))
            s["S"] = s["S"] * s["gl"][c] + lax.dot_general(s["kg16"][rs], vn.astype(BF16), TN_DIMS,
                                                           preferred_element_type=F32)
            s["vn"].append(vn)
    for h, s in zip(heads, st):
        s_sc[h] = s["S"]
        s1_ref[h] = s["S"]
        o = jnp.concatenate(s["oi"], axis=0) + _mm(s["QK16"], jnp.concatenate(s["vn"], axis=0).astype(BF16))
        o = o * lax.rsqrt(jnp.mean(o * o, axis=-1, keepdims=True) + EPS) * nw_ref[...]
        o_ref[:, hsl[h]] = o[0:tb] * _silu(z_ref[:, hsl[h]].astype(F32))


def _gdn(uf3, uh3, conv0, s0, cw, alog_e, dtb_e, nw, tb, l_valid):
    Bn, Lp, _ = uf3.shape
    rows = max(tb, DELTA_CHUNK)
    const2 = lambda b, t: (0, 0)
    return pl.pallas_call(
        functools.partial(_gdn_kernel, tb=tb, l_valid=l_valid),
        grid=(Bn, Lp // tb),
        in_specs=[pl.BlockSpec((None, tb, C_CONV_DIM), lambda b, t: (b, t, HB_QKVC * COL // C_CONV_DIM)),
                  pl.BlockSpec((None, tb, COL), lambda b, t: (b, t, FB_BETA)),
                  pl.BlockSpec((None, tb, COL), lambda b, t: (b, t, FB_ALPHA)),
                  pl.BlockSpec((None, tb, COL), lambda b, t: (b, t, HB_ZC)),
                  pl.BlockSpec((None, CONV_WIDTH - 1, C_CONV_DIM), lambda b, t: (b, 0, 0)),
                  pl.BlockSpec((None, C_HEADS, C_KEY_DIM, C_VAL_DIM), lambda b, t: (b, 0, 0, 0)),
                  pl.BlockSpec((CONV_WIDTH, C_CONV_DIM), const2),
                  pl.BlockSpec((1, C_V_WIDTH), const2),
                  pl.BlockSpec((1, C_V_WIDTH), const2),
                  pl.BlockSpec((1, C_VAL_DIM), const2)],
        out_specs=[pl.BlockSpec((None, tb, C_V_WIDTH), lambda b, t: (b, t, 0)),
                   pl.BlockSpec((None, CONV_WIDTH - 1, C_CONV_DIM), lambda b, t: (b, 0, 0)),
                   pl.BlockSpec((None, C_HEADS, C_KEY_DIM, C_VAL_DIM), lambda b, t: (b, 0, 0, 0))],
        out_shape=[jax.ShapeDtypeStruct((Bn, Lp, C_V_WIDTH), F32),
                   jax.ShapeDtypeStruct((Bn, CONV_WIDTH - 1, C_CONV_DIM), F32),
                   jax.ShapeDtypeStruct((Bn, C_HEADS, C_KEY_DIM, C_VAL_DIM), F32)],
        scratch_shapes=[pltpu.VMEM((SUBLANES + tb, C_CONV_DIM), F32),
                        pltpu.VMEM((C_HEADS, C_KEY_DIM, C_VAL_DIM), F32),
                        pltpu.VMEM((rows, C_CONV_DIM), F32)],
        compiler_params=_cparams("parallel", "arbitrary"),
        name="gdn",
    )(uh3, uf3, uf3, uh3, conv0, s0, cw, alog_e, dtb_e, nw.reshape(1, C_VAL_DIM))


def _merge_kernel(h_ref, oa_ref, za_ref, ob_ref, oc_ref, g0_ref, g1_ref, g2_ref,
                  wpa_ref, wpb_ref, wpc_ref, wout_ref, fnw_ref, hn_ref, *, final_norm):
    def proj(x, w_ref):
        return jnp.dot(x.astype(BF16), w_ref[...], preferred_element_type=F32)

    f32 = lambda ref: ref[...].astype(F32)
    ya = proj(_silu(f32(za_ref)) * oa_ref[...], wpa_ref)
    yb = proj(ob_ref[...], wpb_ref)
    yc = proj(oc_ref[...], wpc_ref)
    merged = _sigmoid(f32(g0_ref)) * ya + _sigmoid(f32(g1_ref)) * yb + _sigmoid(f32(g2_ref)) * yc
    hn = h_ref[...] + proj(merged, wout_ref)
    if final_norm:
        hn = hn * lax.rsqrt(jnp.mean(hn * hn, axis=-1, keepdims=True) + EPS) * fnw_ref[...]
    hn_ref[...] = hn


def _merge(h2d, oa, uh, ob, oc, wpa, wpb, wpc, wout, fnw, tm, final_norm):
    T = h2d.shape[0]
    gpb = D_MODEL // COL
    tok = lambda w: pl.BlockSpec((tm, w), lambda i: (i, 0))
    wspec = lambda k: pl.BlockSpec((k, D_MODEL), lambda i: (0, 0))
    gate_spec = lambda g: pl.BlockSpec((tm, D_MODEL), lambda i: (i, HB_GATES // gpb + g))
    return pl.pallas_call(
        functools.partial(_merge_kernel, final_norm=final_norm),
        grid=(T // tm,),
        in_specs=[tok(D_MODEL), tok(A_WIDTH),
                  pl.BlockSpec((tm, COL), lambda i: (i, HB_ZA)),
                  tok(B_WIDTH), tok(C_V_WIDTH),
                  gate_spec(0), gate_spec(1), gate_spec(2),
                  wspec(A_WIDTH), wspec(B_WIDTH), wspec(C_V_WIDTH), wspec(D_MODEL),
                  pl.BlockSpec((1, D_MODEL), lambda i: (0, 0))],
        out_specs=tok(D_MODEL),
        out_shape=jax.ShapeDtypeStruct((T, D_MODEL), F32),
        compiler_params=_cparams("parallel"),
        name="merge",
    )(h2d, oa, uh, ob, oc, uh, uh, uh, wpa, wpb, wpc, wout, fnw.reshape(1, D_MODEL))


def _block_diag(w):
    g, n, _ = w.shape
    same = jnp.eye(g, dtype=bool)[:, None, :, None]
    return jnp.where(same, w[:, :, None, :], 0.0).reshape(g * n, g * n)


def _rearrange_w_in(w):
    sizes = (A_WIDTH, A_WIDTH, A_WIDTH, A_WIDTH, B_WIDTH, B_WIDTH, C_CONV_DIM, C_V_WIDTH, C_HEADS, C_HEADS,
             N_BRANCH * D_MODEL)
    offs = [0]
    for s in sizes:
        offs.append(offs[-1] + s)
    qa, ka, va, za, xb, zb, qkvc, zc, beta, alpha, gates = (w[:, offs[n]:offs[n + 1]] for n in range(len(sizes)))
    rep = lambda c: jnp.repeat(c, C_VAL_DIM, axis=1)
    return jnp.concatenate([qa, ka, va, rep(beta), rep(alpha), qkvc, za, gates, xb, zb, zc], axis=1).astype(BF16)


def kernel(x_prompt, x_sample, cache_k, cache_v, page_table, state_conv_b, state_h_b, state_conv_c, state_s_c,
           norm_w, w_in, conv_b_w, conv_b_b, lru_wa, lru_ba, lru_wx, lru_bx, lru_lambda,
           conv_c_w, dn_a_log, dn_dt_bias, dn_norm_w, w_pa, w_pb, w_pc, w_out, final_norm_w):
    Bp, L, _ = x_prompt.shape
    Bs, dec_seq, _ = x_sample.shape
    depth = w_in.shape[0]
    n_pool = cache_k.shape[1]
    n_pages = page_table.shape[1]
    past = n_pages * PAGE_SIZE
    tpad = BF16_ROWS
    assert dec_seq <= tpad and L % MOBA_BLOCK == 0 and past % MOBA_BLOCK == 0 and n_pages % PAGES_PER_STEP == 0
    assert MOBA_TOPK <= past // MOBA_BLOCK <= PAGE_SIZE and (dec_seq * A_HEADS) % BF16_ROWS == 0

    slopes = jnp.exp2(-8.0 * jnp.arange(1, A_HEADS + 1, dtype=F32) / A_HEADS)
    row = jnp.arange(dec_seq * A_HEADS)
    slope_rows = jnp.broadcast_to(slopes[row % A_HEADS][:, None], (row.size, LANES))
    tq_rows = jnp.broadcast_to((past + row // A_HEADS).astype(F32)[:, None], (row.size, LANES))
    ck_t = jnp.transpose(cache_k, (0, 1, 3, 4, 2)).reshape(depth, n_pool, A_WIDTH, PAGE_SIZE)
    cv_t = jnp.transpose(cache_v, (0, 1, 3, 4, 2)).reshape(depth, n_pool, A_WIDTH, PAGE_SIZE)

    hp = x_prompt.reshape(Bp * L, D_MODEL)
    hs = jnp.pad(x_sample, ((0, 0), (0, tpad - dec_seq), (0, 0))).reshape(Bs * tpad, D_MODEL)
    zeros = lambda *s: jnp.zeros(s, F32)
    tm_p = 2048 if (Bp * L) % 2048 == 0 else MOBA_BLOCK
    tm_merge = min(tm_p, 512)
    tm_s = Bs * tpad
    tc_p = 256 if L % 256 == 0 else L
    outs = [[] for _ in range(12)]
    for l in range(depth):
        w_re = _rearrange_w_in(w_in[l])
        wa_bd = _block_diag(lru_wa[l]).astype(BF16)
        wx_bd = _block_diag(lru_wx[l]).astype(BF16)
        alog_e = jnp.repeat(dn_a_log[l], C_VAL_DIM).reshape(1, C_V_WIDTH)
        dtb_e = jnp.repeat(dn_dt_bias[l], C_VAL_DIM).reshape(1, C_V_WIDTH)
        wpa, wpb, wpc, wout = (w.astype(BF16) for w in (w_pa[l], w_pb[l], w_pc[l], w_out[l]))
        last = l == depth - 1
        lru_w = (conv_b_w[l], conv_b_b[l], wa_bd, lru_ba[l], wx_bd, lru_bx[l], lru_lambda[l])
        gdn_w = (conv_c_w[l], alog_e, dtb_e, dn_norm_w[l])

        uf, uh = _inproj(hp, norm_w[l], w_re, tm_p)
        uf3 = uf.reshape(Bp, L, N_UF)
        uh3 = uh.reshape(Bp, L, N_UH)
        usf, ush = _inproj(hs, norm_w[l], w_re, tm_s)
        oa, oas = _moba(uf, usf, slopes, slope_rows, tq_rows, ck_t, cv_t, page_table, l, Bp, L, Bs, tpad, dec_seq)
        ob, cbp, hbp = _lru(uh3, zeros(Bp, CONV_WIDTH - 1, B_WIDTH), zeros(Bp, B_WIDTH), *lru_w, tc_p, L)
        oc, ccp, scp = _gdn(uf3, uh3, zeros(Bp, CONV_WIDTH - 1, C_CONV_DIM),
                            zeros(Bp, C_HEADS, C_KEY_DIM, C_VAL_DIM), *gdn_w, tc_p, L)
        hp = _merge(hp, oa, uh, ob.reshape(Bp * L, B_WIDTH), oc.reshape(Bp * L, C_V_WIDTH),
                    wpa, wpb, wpc, wout, final_norm_w, tm_merge, last)
        kp = uf3[:, :, FB_KA * COL:(FB_KA + 1) * COL].reshape(Bp, L, A_HEADS, A_HEAD_DIM)
        vp = uf3[:, :, FB_VA * COL:(FB_VA + 1) * COL].reshape(Bp, L, A_HEADS, A_HEAD_DIM)

        usf3 = usf.reshape(Bs, tpad, N_UF)
        ush3 = ush.reshape(Bs, tpad, N_UH)
        obs, cbs, hbs = _lru(ush3, state_conv_b[l], state_h_b[l], *lru_w, tpad, dec_seq)
        ocs, ccs, scs = _gdn(usf3, ush3, state_conv_c[l], state_s_c[l], *gdn_w, tpad, dec_seq)
        hs = _merge(hs, oas, ush, obs.reshape(Bs * tpad, B_WIDTH), ocs.reshape(Bs * tpad, C_V_WIDTH),
                    wpa, wpb, wpc, wout, final_norm_w, tm_s, last)
        ks = usf3[:, :dec_seq, FB_KA * COL:(FB_KA + 1) * COL].reshape(Bs, dec_seq, A_HEADS, A_HEAD_DIM)
        vs = usf3[:, :dec_seq, FB_VA * COL:(FB_VA + 1) * COL].reshape(Bs, dec_seq, A_HEADS, A_HEAD_DIM)

        hbp = hbp.reshape(Bp, B_WIDTH)
        hbs = hbs.reshape(Bs, B_WIDTH)
        for lst, val in zip(outs, (kp, vp, ks, vs, cbp, hbp, cbs, hbs, ccp, scp, ccs, scs)):
            lst.append(val)

    y_prompt = hp.reshape(Bp, L, D_MODEL)
    y_sample = hs.reshape(Bs, tpad, D_MODEL)[:, :dec_seq]
    return (y_prompt, y_sample) + tuple(jnp.stack(o) for o in outs)
```
